```python
import math
import jax, jax.numpy as jnp
from jax import lax
import numpy as np

D_MODEL = 1024
BATCH = 4
SEQ = 4096
DEPTH = 4

N_MIXERS = 2
N_NSA_LAYERS = (DEPTH + 1) // 2
N_DIFF_LAYERS = DEPTH // 2
HEAD_DIM = 64
ROPE_DIM = HEAD_DIM // 4
ROPE_THETA = 500000.0
NSA_HEADS = D_MODEL // HEAD_DIM
NSA_GROUPS = 4
NSA_HPG = NSA_HEADS // NSA_GROUPS
CMP_BLOCK = 32
CMP_STRIDE = 16
CMP_HIDDEN = 2 * HEAD_DIM
SEL_BLOCK = 64
SEL_TOPK = 16
WINDOW = 512
NSA_Q_CHUNK = 64
NSA_Q_W = NSA_HEADS * HEAD_DIM
NSA_KV_W = NSA_GROUPS * HEAD_DIM
NSA_IN = NSA_Q_W + 6 * NSA_KV_W + 3 * NSA_HEADS
DIFF_HEADS = D_MODEL // (2 * HEAD_DIM)
DIFF_V_DIM = 2 * HEAD_DIM
DIFF_W = DIFF_HEADS * 2 * HEAD_DIM
DIFF_IN = 3 * DIFF_W
ATT_BLOCK = 128
D_FF = 4 * D_MODEL
EPS = 1e-6
NEG_INF = -1e30
SEL_FORCE = 1e6

kernel_name = "hybrid_nsa_diffattn_adaln_trunk"


def rms_norm(x, g):
    xf = x.astype(jnp.float32)
    y = xf * lax.rsqrt(jnp.mean(xf * xf, axis=-1, keepdims=True) + EPS)
    return (y * g.astype(jnp.float32)).astype(x.dtype)


def rope_tables(positions):
    inv = ROPE_THETA ** (-jnp.arange(0, ROPE_DIM, 2, dtype=jnp.float32) / ROPE_DIM)
    ang = positions.astype(jnp.float32)[..., None] * inv
    return jnp.cos(ang), jnp.sin(ang)


def apply_partial_rope(x, cos, sin):
    half = ROPE_DIM // 2
    shape = cos.shape[:2] + (1,) * (x.ndim - 3) + cos.shape[-1:]
    c = cos.reshape(shape)
    s = sin.reshape(shape)
    x1 = x[..., :half].astype(jnp.float32)
    x2 = x[..., half:ROPE_DIM].astype(jnp.float32)
    rot = jnp.concatenate([x1 * c - x2 * s, x2 * c + x1 * s], axis=-1).astype(x.dtype)
    return jnp.concatenate([rot, x[..., ROPE_DIM:]], axis=-1)


def masked_softmax(s, mask):
    return jax.nn.softmax(jnp.where(mask, s, NEG_INF), axis=-1) * mask


def compress_blocks(blk, pe, w1, w2):
    B, NC = blk.shape[:2]
    z = (blk + pe[:, None, :]).transpose(0, 1, 3, 2, 4).reshape(B, NC, NSA_GROUPS, CMP_BLOCK * HEAD_DIM)
    return jax.nn.silu(z @ w1) @ w2


def nsa_mixer(h, cos, sin, w_in, b_gate, q_gain, k_gain, pe_k, w_ck1, w_ck2, pe_v, w_cv1, w_cv2, w_out):
    B, S, _ = h.shape
    G, HPG, dh = NSA_GROUPS, NSA_HPG, HEAD_DIM
    scale = 1.0 / math.sqrt(dh)
    sizes = [NSA_Q_W] + [NSA_KV_W] * 6 + [3 * NSA_HEADS]
    q, kc, vc, ks, vs, kw, vw, gl = jnp.split(h @ w_in, list(np.cumsum(sizes)[:-1]), axis=-1)
    q = rms_norm(q.reshape(B, S, NSA_HEADS, dh), q_gain)
    q_rot = apply_partial_rope(q, cos, sin)
    q_cmp = q.reshape(B, S, G, HPG, dh).transpose(0, 2, 3, 1, 4)
    q_rot = q_rot.reshape(B, S, G, HPG, dh).transpose(0, 2, 3, 1, 4)
    gates = jax.nn.sigmoid(gl.astype(jnp.float32) + b_gate.astype(jnp.float32)).reshape(B, S, NSA_HEADS, 3)

    n_cmp = (S - CMP_BLOCK) // CMP_STRIDE + 1
    idx = jnp.arange(n_cmp)[:, None] * CMP_STRIDE + jnp.arange(CMP_BLOCK)[None, :]
    kc = kc.reshape(B, S, G, dh)
    vc = vc.reshape(B, S, G, dh)
    k_cmp = rms_norm(compress_blocks(kc[:, idx], pe_k, w_ck1, w_ck2), k_gain[0]).transpose(0, 2, 1, 3)
    v_cmp = compress_blocks(vc[:, idx], pe_v, w_cv1, w_cv2).transpose(0, 2, 1, 3)
    cmp_start = jnp.arange(n_cmp) * CMP_STRIDE
    cmp_end = cmp_start + CMP_BLOCK - 1
    n_sel = S // SEL_BLOCK
    top_n = min(SEL_TOPK, n_sel)
    sel_start = jnp.arange(n_sel) * SEL_BLOCK
    overlap = ((cmp_start[:, None] <= sel_start[None, :] + SEL_BLOCK - 1)
               & (cmp_end[:, None] >= sel_start[None, :])).astype(jnp.float32)

    ks = apply_partial_rope(rms_norm(ks.reshape(B, S, G, dh), k_gain[1]), cos, sin)
    ks_blk = ks.transpose(0, 2, 1, 3).reshape(B, G, n_sel, SEL_BLOCK, dh)
    vs_blk = vs.reshape(B, S, G, dh).transpose(0, 2, 1, 3).reshape(B, G, n_sel, SEL_BLOCK, dh)
    kw = apply_partial_rope(rms_norm(kw.reshape(B, S, G, dh), k_gain[2]), cos, sin)
    pad = ((0, 0), (0, 0), (WINDOW, 0), (0, 0))
    kw_pad = jnp.pad(kw.transpose(0, 2, 1, 3), pad)
    vw_pad = jnp.pad(vw.reshape(B, S, G, dh).transpose(0, 2, 1, 3), pad)
    b_idx = jnp.arange(B)[:, None, None, None]
    g_idx = jnp.arange(G)[None, :, None, None]
    blk_ids = jnp.arange(n_sel)
    QC = NSA_Q_CHUNK

    def chunk(ci):
        s0 = ci * QC
        t = s0 + jnp.arange(QC)
        qc_ = lax.dynamic_slice_in_dim(q_cmp, s0, QC, axis=3)
        qr_ = lax.dynamic_slice_in_dim(q_rot, s0, QC, axis=3)
        sc = jnp.einsum('bghqd,bgcd->bghqc', qc_, k_cmp).astype(jnp.float32) * scale
        pc = masked_softmax(sc, cmp_end[None, :] <= t[:, None])
        o_cmp = jnp.einsum('bghqc,bgcd->bghqd', pc.astype(v_cmp.dtype), v_cmp)
        imp = jnp.einsum('bghqc,cs->bgqs', pc, overlap)
        bt = (t // SEL_BLOCK)[:, None]
        valid = blk_ids[None, :] <= bt
        forced = (blk_ids[None, :] == 0) | (blk_ids[None, :] == bt) | (blk_ids[None, :] == bt - 1)
        imp = jnp.where(forced, SEL_FORCE, jnp.where(valid, imp, -1.0))
        _, sel = lax.top_k(imp, top_n)
        kg = ks_blk[b_idx, g_idx, sel]
        vg = vs_blk[b_idx, g_idx, sel]
        ss = jnp.einsum('bghqd,bgqnld->bghqnl', qr_, kg).astype(jnp.float32) * scale
        tok = sel[..., None] * SEL_BLOCK + jnp.arange(SEL_BLOCK)
        ms = (tok <= t[:, None, None])[:, :, None]
        shp = ss.shape
        ps = masked_softmax(ss.reshape(shp[:4] + (top_n * SEL_BLOCK,)),
                            ms.reshape(ms.shape[:4] + (top_n * SEL_BLOCK,))).reshape(shp)
        o_sel = jnp.einsum('bghqnl,bgqnld->bghqd', ps.astype(vg.dtype), vg)
        kw_ = lax.dynamic_slice_in_dim(kw_pad, s0, WINDOW + QC, axis=2)
        vw_ = lax.dynamic_slice_in_dim(vw_pad, s0, WINDOW + QC, axis=2)
        kpos = s0 - WINDOW + jnp.arange(WINDOW + QC)
        mw = (kpos[None, :] <= t[:, None]) & (kpos[None, :] > t[:, None] - WINDOW) & (kpos[None, :] >= 0)
        sw = jnp.einsum('bghqd,bgkd->bghqk', qr_, kw_).astype(jnp.float32) * scale
        o_win = jnp.einsum('bghqk,bgkd->bghqd', masked_softmax(sw, mw).astype(vw_.dtype), vw_)
        g = lax.dynamic_slice_in_dim(gates, s0, QC, axis=1)
        g = g.reshape(B, QC, G, HPG, 3).transpose(0, 2, 3, 1, 4).astype(o_win.dtype)
        o = g[..., 0:1] * o_cmp + g[..., 1:2] * o_sel + g[..., 2:3] * o_win
        return o.transpose(0, 3, 1, 2, 4).reshape(B, QC, NSA_Q_W)

    outs = lax.map(chunk, jnp.arange(S // QC))
    return outs.transpose(1, 0, 2, 3).reshape(B, S, NSA_Q_W) @ w_out


def diff_mixer(h, cos, sin, layer_depth, w_in, q_gain, k_gain, lq1, lk1, lq2, lk2, subln_g, w_out):
    B, S, _ = h.shape
    dh = HEAD_DIM
    scale = 1.0 / math.sqrt(dh)
    q, k, v = jnp.split(h @ w_in, [DIFF_W, 2 * DIFF_W], axis=-1)
    q = apply_partial_rope(rms_norm(q.reshape(B, S, DIFF_HEADS, 2, dh), q_gain), cos, sin)
    k = apply_partial_rope(rms_norm(k.reshape(B, S, DIFF_HEADS, 2, dh), k_gain), cos, sin)
    q = q.transpose(0, 2, 3, 1, 4)
    k = k.transpose(0, 2, 3, 1, 4)
    v = v.reshape(B, S, DIFF_HEADS, DIFF_V_DIM).transpose(0, 2, 1, 3)
    lam_init = 0.8 - 0.6 * math.exp(-0.3 * (layer_depth - 1))
    lam = (jnp.exp(jnp.sum(lq1.astype(jnp.float32) * lk1.astype(jnp.float32)))
           - jnp.exp(jnp.sum(lq2.astype(jnp.float32) * lk2.astype(jnp.float32))) + lam_init)
    kpos = jnp.arange(S)

    def block(bi):
        s0 = bi * ATT_BLOCK
        qb = lax.dynamic_slice_in_dim(q, s0, ATT_BLOCK, axis=3)
        s = jnp.einsum('bhcqd,bhckd->bhcqk', qb, k).astype(jnp.float32) * scale
        t = s0 + jnp.arange(ATT_BLOCK)
        p = masked_softmax(s, kpos[None, :] <= t[:, None])
        a = p[:, :, 0] - lam * p[:, :, 1]
        o = jnp.einsum('bhqk,bhkd->bhqd', a.astype(v.dtype), v)
        o = rms_norm(o, subln_g) * (1.0 - lam_init)
        return o.transpose(0, 2, 1, 3).reshape(B, ATT_BLOCK, DIFF_W)

    outs = lax.map(block, jnp.arange(S // ATT_BLOCK))
    return outs.transpose(1, 0, 2, 3).reshape(B, S, DIFF_W) @ w_out


def setup_inputs(seed: int = 0) -> dict:
    key = jax.random.key(seed)
    ks = iter(jax.random.split(key, 32))
    D = D_MODEL

    def nrm(shape, scale):
        return jax.random.normal(next(ks), shape, jnp.float32) * scale

    def gain(shape):
        return 1.0 + nrm(shape, 0.02)

    x = nrm((BATCH, SEQ, D), 1.0)
    c = nrm((BATCH, D), 1.0)
    offsets = jax.random.randint(next(ks), (BATCH, 1), 0, 1024, dtype=jnp.int32)
    positions = (offsets + jnp.arange(SEQ, dtype=jnp.int32)[None, :]).astype(jnp.int32)
    NL, NDF = N_NSA_LAYERS, N_DIFF_LAYERS
    return {
        'x': x, 'c': c, 'positions': positions,
        'ln_mix_g': gain((DEPTH, D)), 'ln_mlp_g': gain((DEPTH, D)),
        'w_ada': nrm((DEPTH, D, 6 * D), D ** -0.5), 'b_ada': nrm((DEPTH, 6 * D), 0.01),
        'w_mlp_in': nrm((DEPTH, D, D_FF), D ** -0.5), 'w_mlp_out': nrm((DEPTH, D_FF, D), D_FF ** -0.5),
        'nsa_w_in': nrm((NL, D, NSA_IN), D ** -0.5), 'nsa_b_gate': nrm((NL, 3 * NSA_HEADS), 0.01),
        'nsa_q_gain': gain((NL, HEAD_DIM)), 'nsa_k_gain': gain((NL, 3, HEAD_DIM)),
        'nsa_pe_k': nrm((NL, CMP_BLOCK, HEAD_DIM), 0.1),
        'nsa_w_ck1': nrm((NL, CMP_BLOCK * HEAD_DIM, CMP_HIDDEN), (CMP_BLOCK * HEAD_DIM) ** -0.5),
        'nsa_w_ck2': nrm((NL, CMP_HIDDEN, HEAD_DIM), CMP_HIDDEN ** -0.5),
        'nsa_pe_v': nrm((NL, CMP_BLOCK, HEAD_DIM), 0.1),
        'nsa_w_cv1': nrm((NL, CMP_BLOCK * HEAD_DIM, CMP_HIDDEN), (CMP_BLOCK * HEAD_DIM) ** -0.5),
        'nsa_w_cv2': nrm((NL, CMP_HIDDEN, HEAD_DIM), CMP_HIDDEN ** -0.5),
        'nsa_w_out': nrm((NL, NSA_Q_W, D), NSA_Q_W ** -0.5),
        'diff_w_in': nrm((NDF, D, DIFF_IN), D ** -0.5),
        'diff_q_gain': gain((NDF, HEAD_DIM)), 'diff_k_gain': gain((NDF, HEAD_DIM)),
        'diff_lq1': nrm((NDF, HEAD_DIM), 0.1), 'diff_lk1': nrm((NDF, HEAD_DIM), 0.1),
        'diff_lq2': nrm((NDF, HEAD_DIM), 0.1), 'diff_lk2': nrm((NDF, HEAD_DIM), 0.1),
        'diff_subln_g': gain((NDF, DIFF_V_DIM)),
        'diff_w_out': nrm((NDF, DIFF_W, D), DIFF_W ** -0.5),
    }


def reference(x, c, positions, ln_mix_g, ln_mlp_g, w_ada, b_ada, w_mlp_in, w_mlp_out,
              nsa_w_in, nsa_b_gate, nsa_q_gain, nsa_k_gain, nsa_pe_k, nsa_w_ck1, nsa_w_ck2,
              nsa_pe_v, nsa_w_cv1, nsa_w_cv2, nsa_w_out,
              diff_w_in, diff_q_gain, diff_k_gain, diff_lq1, diff_lk1, diff_lq2, diff_lk2,
              diff_subln_g, diff_w_out):
    cos, sin = rope_tables(positions)
    cond = jax.nn.silu(c)
    for i in range(DEPTH):
        mod = cond @ w_ada[i] + b_ada[i]
        sh1, sc1, g1, sh2, sc2, g2 = jnp.split(mod, 6, axis=-1)
        h = rms_norm(x, ln_mix_g[i]) * (1.0 + sc1[:, None, :]) + sh1[:, None, :]
        j = i // N_MIXERS
        if i % N_MIXERS == 0:
            y = nsa_mixer(h, cos, sin, nsa_w_in[j], nsa_b_gate[j], nsa_q_gain[j], nsa_k_gain[j],
                          nsa_pe_k[j], nsa_w_ck1[j], nsa_w_ck2[j], nsa_pe_v[j], nsa_w_cv1[j],
                          nsa_w_cv2[j], nsa_w_out[j])
        else:
            y = diff_mixer(h, cos, sin, i + 1, diff_w_in[j], diff_q_gain[j], diff_k_gain[j],
                           diff_lq1[j], diff_lk1[j], diff_lq2[j], diff_lk2[j], diff_subln_g[j],
                           diff_w_out[j])
        x = x + g1[:, None, :] * y
        h = rms_norm(x, ln_mlp_g[i]) * (1.0 + sc2[:, None, :]) + sh2[:, None, :]
        x = x + g2[:, None, :] * (jnp.square(jax.nn.relu(h @ w_mlp_in[i])) @ w_mlp_out[i])
    return x
```

```python
import functools
import math

import jax
import jax.numpy as jnp
from jax import lax
from jax.experimental import pallas as pl
from jax.experimental.pallas import tpu as pltpu

F32 = jnp.float32
BF16 = jnp.bfloat16

LANES = 128
HEAD_DIM = 64
ROPE_DIM = HEAD_DIM // 4
ROPE_THETA = 500000.0
NSA_GROUPS = 4
NSA_HPG = 4
NSA_HEADS = NSA_GROUPS * NSA_HPG
CMP_BLOCK = 32
CMP_STRIDE = 16
CMP_HIDDEN = 2 * HEAD_DIM
SEL_BLOCK = 64
SEL_TOPK = 16
WINDOW = 512
EPS = 1e-6
NEG_INF = -1e30
SEL_FORCE = 1e6
QK_SCALE = 1.0 / math.sqrt(HEAD_DIM)

VMEM_LIMIT = 48 * 1024 * 1024


def _dot(a, b):
    return jnp.dot(a, b, preferred_element_type=F32)


def _dot_nt(a, b):
    return lax.dot_general(a, b, (((1,), (1,)), ((), ())), preferred_element_type=F32)


def _sigmoid(x):
    return 1.0 / (1.0 + jnp.exp(-x))


def _split3(x):
    hi = x.astype(BF16)
    r1 = x - hi.astype(F32)
    mid = r1.astype(BF16)
    lo = (r1 - mid.astype(F32)).astype(BF16)
    return hi, mid, lo


def _mod_rmsnorm(x, g_ln, shift, scale):
    ms = jnp.mean(x * x, axis=-1, keepdims=True)
    return (x * lax.rsqrt(ms + EPS) * g_ln) * (1.0 + scale) + shift


def _head_rmsnorm(a, e_ref, et_ref, gain):
    ss = _dot((a * a).astype(BF16), e_ref[...])
    r = lax.rsqrt(ss * (1.0 / HEAD_DIM) + EPS)
    r_hi = r.astype(BF16)
    r_lo = (r - r_hi.astype(F32)).astype(BF16)
    et = et_ref[...]
    return a * (_dot(r_hi, et) + _dot(r_lo, et)) * gain


def _rope(a, cosf, sinf):
    lane = lax.broadcasted_iota(jnp.int32, (a.shape[0], LANES), 1)
    first = (lane & (HEAD_DIM - 1)) < (ROPE_DIM // 2)
    half = ROPE_DIM // 2
    outs = []
    for j in range(a.shape[1] // LANES):
        blk = a[:, j * LANES:(j + 1) * LANES]
        partner = jnp.where(first, pltpu.roll(blk, LANES - half, 1), pltpu.roll(blk, half, 1))
        outs.append(blk * cosf + partner * sinf)
    return jnp.concatenate(outs, axis=1)


def _ada_kernel(c_ref, w_ref, b_ref, o_ref):
    c = c_ref[...]
    cond = c * _sigmoid(c)
    w = w_ref[0]
    c_hi = cond.astype(BF16)
    c_lo = (cond - c_hi.astype(F32)).astype(BF16)
    w_hi = w.astype(BF16)
    w_lo = (w - w_hi.astype(F32)).astype(BF16)
    o_ref[0] = _dot(c_hi, w_hi) + _dot(c_lo, w_hi) + _dot(c_hi, w_lo) + b_ref[0]


def _ada_mod(c_pad, w_ada, b_ada):
    depth, d, n = w_ada.shape
    tn = 1536
    return pl.pallas_call(
        _ada_kernel,
        grid=(depth, n // tn),
        in_specs=[
            pl.BlockSpec((8, d), lambda l, j: (0, 0)),
            pl.BlockSpec((1, d, tn), lambda l, j: (l, 0, j)),
            pl.BlockSpec((1, 1, tn), lambda l, j: (l, 0, j)),
        ],
        out_specs=pl.BlockSpec((1, 8, tn), lambda l, j: (l, 0, j)),
        out_shape=jax.ShapeDtypeStruct((depth, 8, n), F32),
        compiler_params=pltpu.CompilerParams(
            dimension_semantics=("arbitrary", "arbitrary"), vmem_limit_bytes=VMEM_LIMIT),
        name="ada_mod",
    )(c_pad, w_ada, b_ada.reshape(depth, 1, n))


def _nsa_proj_kernel(x_ref, mod_ref, gln_ref, wn_ref, wr_ref, wg_ref, bg_ref, e_ref, et_ref,
                     gain_ref, cos_ref, sin_ref,
                     qn_ref, qr_ref, ks_ref, kw_ref, kc_ref, vc_ref, vs_ref, vw_ref, gate_ref):
    m = mod_ref[0]
    h = _mod_rmsnorm(x_ref[...], gln_ref[...], m[0:1], m[1:2]).astype(BF16)
    a = _dot(h, wn_ref[...])
    an = _head_rmsnorm(a, e_ref, et_ref, gain_ref[...])
    ar = _rope(an, cos_ref[...], sin_ref[...])
    nq = qn_ref.shape[1]
    nk = ks_ref.shape[1]
    qn_ref[...] = (an[:, :nq] * QK_SCALE).astype(BF16)
    qr_ref[...] = (ar[:, :nq] * QK_SCALE).astype(BF16)
    ks_ref[...] = ar[:, nq:nq + nk].astype(BF16)
    kw_ref[...] = ar[:, nq + nk:nq + 2 * nk].astype(BF16)
    raw = _dot(h, wr_ref[...])
    kc_ref[...] = raw[:, 0:nk]
    vc_ref[...] = raw[:, nk:2 * nk]
    vs_ref[...] = raw[:, 2 * nk:3 * nk].astype(BF16)
    vw_ref[...] = raw[:, 3 * nk:4 * nk].astype(BF16)
    gate_ref[...] = _sigmoid(_dot(h, wg_ref[...]) + bg_ref[...])


def _diff_proj_kernel(x_ref, mod_ref, gln_ref, wn_ref, wr_ref, e_ref, et_ref, gain_ref,
                      cos_ref, sin_ref, q_ref, k_ref, v_ref):
    m = mod_ref[0]
    h = _mod_rmsnorm(x_ref[...], gln_ref[...], m[0:1], m[1:2]).astype(BF16)
    a = _dot(h, wn_ref[...])
    ar = _rope(_head_rmsnorm(a, e_ref, et_ref, gain_ref[...]), cos_ref[...], sin_ref[...])
    nq = q_ref.shape[1]
    q_ref[...] = (ar[:, :nq] * QK_SCALE).astype(BF16)
    k_ref[...] = ar[:, nq:].astype(BF16)
    v_ref[...] = _dot(h, wr_ref[...]).astype(BF16)


def _row_spec(tm, n):
    return pl.BlockSpec((tm, n), lambda i: (i, 0))


def _full_spec(shape):
    return pl.BlockSpec(shape, lambda i: (0,) * len(shape))


def _head_indicator(n):
    head = jnp.arange(n)[:, None] // HEAD_DIM
    e = (head == jnp.arange(LANES)[None, :]).astype(BF16)
    return e, e.T


def _nsa_proj(x2, mod, gln, wn, wr, wg, bg, gain, cosf, sinf, seq, tm):
    t, d = x2.shape
    nn, nr = wn.shape[1], wr.shape[1]
    nq, nk = NSA_HEADS * HEAD_DIM, NSA_GROUPS * HEAD_DIM
    e, et = _head_indicator(nn)
    outs = [jax.ShapeDtypeStruct((t, nq), BF16), jax.ShapeDtypeStruct((t, nq), BF16),
            jax.ShapeDtypeStruct((t, nk), BF16), jax.ShapeDtypeStruct((t, nk), BF16),
            jax.ShapeDtypeStruct((t, nk), F32), jax.ShapeDtypeStruct((t, nk), F32),
            jax.ShapeDtypeStruct((t, nk), BF16), jax.ShapeDtypeStruct((t, nk), BF16),
            jax.ShapeDtypeStruct((t, LANES), F32)]
    return pl.pallas_call(
        _nsa_proj_kernel,
        grid=(t // tm,),
        in_specs=[
            _row_spec(tm, d),
            pl.BlockSpec((1, 8, d), lambda i: ((i * tm) // seq, 0, 0)),
            _full_spec((1, d)), _full_spec((d, nn)), _full_spec((d, nr)), _full_spec((d, LANES)),
            _full_spec((1, LANES)), _full_spec((nn, LANES)), _full_spec((LANES, nn)),
            _full_spec((1, nn)), _row_spec(tm, LANES), _row_spec(tm, LANES),
        ],
        out_specs=[_row_spec(tm, o.shape[1]) for o in outs],
        out_shape=outs,
        compiler_params=pltpu.CompilerParams(
            dimension_semantics=("arbitrary",), vmem_limit_bytes=VMEM_LIMIT),
        name="nsa_proj",
    )(x2, mod, gln, wn, wr, wg, bg, e, et, gain, cosf, sinf)


def _diff_proj(x2, mod, gln, wn, wr, gain, cosf, sinf, seq, tm):
    t, d = x2.shape
    nn, nr = wn.shape[1], wr.shape[1]
    e, et = _head_indicator(nn)
    outs = [jax.ShapeDtypeStruct((t, nn // 2), BF16), jax.ShapeDtypeStruct((t, nn // 2), BF16),
            jax.ShapeDtypeStruct((t, nr), BF16)]
    return pl.pallas_call(
        _diff_proj_kernel,
        grid=(t // tm,),
        in_specs=[
            _row_spec(tm, d),
            pl.BlockSpec((1, 8, d), lambda i: ((i * tm) // seq, 0, 0)),
            _full_spec((1, d)), _full_spec((d, nn)), _full_spec((d, nr)),
            _full_spec((nn, LANES)), _full_spec((LANES, nn)), _full_spec((1, nn)),
            _row_spec(tm, LANES), _row_spec(tm, LANES),
        ],
        out_specs=[_row_spec(tm, o.shape[1]) for o in outs],
        out_shape=outs,
        compiler_params=pltpu.CompilerParams(
            dimension_semantics=("arbitrary",), vmem_limit_bytes=VMEM_LIMIT),
        name="diff_proj",
    )(x2, mod, gln, wn, wr, e, et, gain, cosf, sinf)


def _compress_one(x, pe_ref, w1_ref, w2_ref):
    half = x.shape[1]
    za = (x + pe_ref[0:1, :]).astype(BF16)
    zb = (x + pe_ref[1:2, :]).astype(BF16)
    a = _dot(za, w1_ref[0:half, :])
    b = _dot(zb, w1_ref[half:2 * half, :])
    pre = a + pltpu.roll(b, x.shape[0] - 1, 0)
    hid = pre * _sigmoid(pre)
    return _dot(hid.astype(BF16), w2_ref[...])


def _compress_kernel(xk_ref, xv_ref, pek_ref, pev_ref, wk1_ref, wk2_ref, wv1_ref, wv2_ref,
                     kg_ref, kc_ref, vc_ref):
    k = _compress_one(xk_ref[0, 0], pek_ref, wk1_ref, wk2_ref)
    ms = jnp.mean(k * k, axis=-1, keepdims=True)
    kc_ref[0, 0] = (k * lax.rsqrt(ms + EPS) * kg_ref[...]).astype(BF16)
    vc_ref[0, 0] = _compress_one(xv_ref[0, 0], pev_ref, wv1_ref, wv2_ref).astype(BF16)


def _nsa_compress(xk, xv, pek, pev, wk1, wk2, wv1, wv2, kg0):
    b, g, nh, w = xk.shape
    blk = pl.BlockSpec((1, 1, nh, w), lambda i, j: (i, j, 0, 0))
    full = lambda s: pl.BlockSpec(s, lambda i, j: (0,) * len(s))
    out = jax.ShapeDtypeStruct((b, g, nh, HEAD_DIM), BF16)
    oblk = pl.BlockSpec((1, 1, nh, HEAD_DIM), lambda i, j: (i, j, 0, 0))
    return pl.pallas_call(
        _compress_kernel,
        grid=(b, g),
        in_specs=[blk, blk, full(pek.shape), full(pev.shape), full(wk1.shape), full(wk2.shape),
                  full(wv1.shape), full(wv2.shape), full(kg0.shape)],
        out_specs=[oblk, oblk],
        out_shape=[out, out],
        compiler_params=pltpu.CompilerParams(
            dimension_semantics=("arbitrary", "arbitrary"), vmem_limit_bytes=VMEM_LIMIT),
        name="nsa_compress",
    )(xk, xv, pek, pev, wk1, wk2, wv1, wv2, kg0)


def _nsa_attn_kernel(qn_ref, qr_ref, gate_ref, kc_ref, vc_ref, ks_ref, vs_ref, kw_ref, vw_ref,
                     ov_ref, xp_ref, o_ref, bias_ref, m_ref, l_ref, acc_ref, *, tq, tk):
    g = pl.program_id(1)
    q0 = pl.program_id(2) * tq
    ncb = kc_ref.shape[2]
    t = q0 + lax.broadcasted_iota(jnp.int32, (tq, 1), 0)

    kc = kc_ref[0, 0]
    vc = vc_ref[0, 0]
    cend = lax.broadcasted_iota(jnp.int32, (tq, ncb), 1) * CMP_STRIDE + (CMP_BLOCK - 1)
    cmask = cend <= t
    pc_sum = jnp.zeros((tq, ncb), F32)
    o_cmp = []
    for j in range(NSA_HPG):
        s = jnp.where(cmask, _dot_nt(qn_ref[:, j * HEAD_DIM:(j + 1) * HEAD_DIM], kc), NEG_INF)
        p = jnp.where(cmask, jnp.exp(s - jnp.max(s, axis=-1, keepdims=True)), 0.0)
        l = jnp.sum(p, axis=-1, keepdims=True)
        p = p * (1.0 / jnp.where(l > 0.0, l, 1.0))
        pc_sum = pc_sum + p
        o_cmp.append(_dot(p.astype(BF16), vc))

    ov = ov_ref[...]
    imp = sum(_dot(part, ov) for part in _split3(pc_sum))
    blk = lax.broadcasted_iota(jnp.int32, (tq, LANES), 1)
    bt = lax.shift_right_arithmetic(t, int(math.log2(SEL_BLOCK)))
    n_sel = xp_ref.shape[1] // SEL_BLOCK
    forced = (blk == 0) | (blk == bt) | (blk == bt - 1)
    val = jnp.where(forced, SEL_FORCE, jnp.where(blk <= bt, imp, -1.0))
    val = jnp.where(blk < n_sel, val, -2.0)
    rank = jnp.zeros((tq, LANES), F32)
    for jp in range(n_sel):
        col = val[:, jp:jp + 1]
        beats = (col > val) | ((col == val) & (blk > jp))
        rank = rank + jnp.where(beats, 1.0, 0.0)
    sel = jnp.where((rank < float(min(SEL_TOPK, n_sel))) & (blk < n_sel), 1.0, 0.0).astype(BF16)
    selx = _dot(sel, xp_ref[...])
    kpos = lax.broadcasted_iota(jnp.int32, selx.shape, 1)
    bias_ref[...] = jnp.where((selx > 0.5) & (kpos <= t), 0.0, NEG_INF)

    m_ref[...] = jnp.full(m_ref.shape, NEG_INF, F32)
    l_ref[...] = jnp.zeros(l_ref.shape, F32)
    acc_ref[...] = jnp.zeros(acc_ref.shape, F32)

    def sel_step(jt, carry):
        k0 = pl.multiple_of(jt * tk, tk)
        kt = ks_ref[0, 0, pl.ds(k0, tk), :]
        vt = vs_ref[0, 0, pl.ds(k0, tk), :]
        bias = bias_ref[:, pl.ds(k0, tk)]
        for j in range(NSA_HPG):
            s = _dot_nt(qr_ref[:, j * HEAD_DIM:(j + 1) * HEAD_DIM], kt) + bias
            m_old = m_ref[j]
            m_new = jnp.maximum(m_old, jnp.max(s, axis=-1, keepdims=True))
            alpha = jnp.exp(m_old - m_new)
            p = jnp.exp(s - m_new)
            l_ref[j] = alpha * l_ref[j] + jnp.sum(p, axis=-1, keepdims=True)
            acc_ref[j] = alpha * acc_ref[j] + _dot(p.astype(BF16), vt)
            m_ref[j] = m_new
        return carry

    lax.fori_loop(0, (q0 + tq + tk - 1) // tk, sel_step, 0)

    start = pl.multiple_of(jnp.maximum(q0 - WINDOW, 0), tq)
    kwt = kw_ref[0, 0, pl.ds(start, WINDOW + tq), :]
    vwt = vw_ref[0, 0, pl.ds(start, WINDOW + tq), :]
    wpos = start + lax.broadcasted_iota(jnp.int32, (tq, WINDOW + tq), 1)
    wmask = (wpos <= t) & (wpos > t - WINDOW)

    gates = gate_ref[...]
    glane = lax.broadcasted_iota(jnp.int32, gates.shape, 1)

    def gate_col(c):
        return jnp.sum(jnp.where(glane == c, gates, 0.0), axis=-1, keepdims=True)

    for j in range(NSA_HPG):
        s = jnp.where(wmask, _dot_nt(qr_ref[:, j * HEAD_DIM:(j + 1) * HEAD_DIM], kwt), NEG_INF)
        p = jnp.exp(s - jnp.max(s, axis=-1, keepdims=True))
        o_win = _dot(p.astype(BF16), vwt) * (1.0 / jnp.sum(p, axis=-1, keepdims=True))
        o_sel = acc_ref[j] * (1.0 / l_ref[j])
        head = g * NSA_HPG + j
        o = (gate_col(head) * o_cmp[j] + gate_col(NSA_HEADS + head) * o_sel
             + gate_col(2 * NSA_HEADS + head) * o_win)
        o_ref[:, j * HEAD_DIM:(j + 1) * HEAD_DIM] = o.astype(BF16)


def _nsa_attn(qn, qr, gates, kcmp, vcmp, ks, vs, kw, vw, ov, xp, tq, tk):
    b, g, s, _ = ks.shape
    t = qn.shape[0]
    gw = NSA_HPG * HEAD_DIM
    nq = s // tq
    qspec = pl.BlockSpec((tq, gw), lambda bi, gi, i: (bi * nq + i, gi))
    kv = lambda n: pl.BlockSpec((1, 1, n, HEAD_DIM), lambda bi, gi, i: (bi, gi, 0, 0))
    full = lambda shp: pl.BlockSpec(shp, lambda bi, gi, i: (0,) * len(shp))
    return pl.pallas_call(
        functools.partial(_nsa_attn_kernel, tq=tq, tk=tk),
        grid=(b, g, nq),
        in_specs=[qspec, qspec,
                  pl.BlockSpec((tq, LANES), lambda bi, gi, i: (bi * nq + i, 0)),
                  kv(kcmp.shape[2]), kv(kcmp.shape[2]), kv(s), kv(s), kv(s), kv(s),
                  full(ov.shape), full(xp.shape)],
        out_specs=qspec,
        out_shape=jax.ShapeDtypeStruct((t, g * gw), BF16),
        scratch_shapes=[pltpu.VMEM((tq, s), F32),
                        pltpu.VMEM((NSA_HPG, tq, 1), F32),
                        pltpu.VMEM((NSA_HPG, tq, 1), F32),
                        pltpu.VMEM((NSA_HPG, tq, HEAD_DIM), F32)],
        compiler_params=pltpu.CompilerParams(
            dimension_semantics=("arbitrary", "arbitrary", "arbitrary"),
            vmem_limit_bytes=VMEM_LIMIT),
        name="nsa_attn",
    )(qn, qr, gates, kcmp, vcmp, ks, vs, kw, vw, ov, xp)


def _diff_attn_kernel(lamv_ref, q_ref, k_ref, v_ref, sg_ref, o_ref, m_ref, l_ref, acc_ref,
                      *, tq, lam_init):
    q0 = pl.program_id(2) * tq
    lv = lamv_ref[...]
    lam = (jnp.exp(jnp.sum(lv[0:1] * lv[1:2], axis=-1, keepdims=True))
           - jnp.exp(jnp.sum(lv[2:3] * lv[3:4], axis=-1, keepdims=True)) + lam_init)
    q = q_ref[...]
    lane = lax.broadcasted_iota(jnp.int32, q.shape, 1)
    zero = jnp.zeros_like(q)
    qc = (jnp.where(lane < HEAD_DIM, q, zero), jnp.where(lane >= HEAD_DIM, q, zero))

    m_ref[...] = jnp.full(m_ref.shape, NEG_INF, F32)
    l_ref[...] = jnp.zeros(l_ref.shape, F32)
    acc_ref[...] = jnp.zeros(acc_ref.shape, F32)

    def step(k0, mask):
        kt = k_ref[0, pl.ds(k0, tq), :]
        vt = v_ref[0, pl.ds(k0, tq), :]
        for c in range(2):
            s = _dot_nt(qc[c], kt)
            if mask is not None:
                s = jnp.where(mask, s, NEG_INF)
            m_old = m_ref[c]
            m_new = jnp.maximum(m_old, jnp.max(s, axis=-1, keepdims=True))
            alpha = jnp.exp(m_old - m_new)
            p = jnp.exp(s - m_new)
            l_ref[c] = alpha * l_ref[c] + jnp.sum(p, axis=-1, keepdims=True)
            acc_ref[c] = alpha * acc_ref[c] + _dot(p.astype(BF16), vt)
            m_ref[c] = m_new

    def full_step(jt, carry):
        step(pl.multiple_of(jt * tq, tq), None)
        return carry

    lax.fori_loop(0, pl.program_id(2), full_step, 0)
    row = lax.broadcasted_iota(jnp.int32, (tq, tq), 0)
    col = lax.broadcasted_iota(jnp.int32, (tq, tq), 1)
    step(pl.multiple_of(q0, tq), col <= row)

    o = acc_ref[0] * (1.0 / l_ref[0]) - lam * (acc_ref[1] * (1.0 / l_ref[1]))
    ms = jnp.mean(o * o, axis=-1, keepdims=True)
    o_ref[...] = ((o * lax.rsqrt(ms + EPS) * sg_ref[...]) * (1.0 - lam_init)).astype(BF16)


def _diff_attn(lamv, q, k, v, sg, tq, lam_init):
    b, s, w = k.shape
    heads = w // (2 * HEAD_DIM)
    hw = 2 * HEAD_DIM
    nq = s // tq
    full = lambda shp: pl.BlockSpec(shp, lambda bi, hi, i: (0,) * len(shp))
    qspec = pl.BlockSpec((tq, hw), lambda bi, hi, i: (bi * nq + i, hi))
    kvspec = pl.BlockSpec((1, s, hw), lambda bi, hi, i: (bi, 0, hi))
    return pl.pallas_call(
        functools.partial(_diff_attn_kernel, tq=tq, lam_init=lam_init),
        grid=(b, heads, nq),
        in_specs=[full(lamv.shape), qspec, kvspec, kvspec, full(sg.shape)],
        out_specs=qspec,
        out_shape=jax.ShapeDtypeStruct(q.shape, BF16),
        scratch_shapes=[pltpu.VMEM((2, tq, 1), F32), pltpu.VMEM((2, tq, 1), F32),
                        pltpu.VMEM((2, tq, hw), F32)],
        compiler_params=pltpu.CompilerParams(
            dimension_semantics=("arbitrary", "arbitrary", "arbitrary"),
            vmem_limit_bytes=VMEM_LIMIT),
        name="diff_attn",
    )(lamv, q, k, v, sg)


def _mlp_kernel(x_ref, a_ref, mod_ref, gln_ref, wo_ref, w1_ref, w2_ref, o_ref,
                x1_ref, h_ref, acc_ref):
    kf = pl.program_id(1)
    m = mod_ref[0]

    @pl.when(kf == 0)
    def _():
        x1 = x_ref[...] + m[2:3] * _dot(a_ref[...], wo_ref[...])
        x1_ref[...] = x1
        h_ref[...] = _mod_rmsnorm(x1, gln_ref[...], m[3:4], m[4:5]).astype(BF16)
        acc_ref[...] = jnp.zeros(acc_ref.shape, F32)

    u = jnp.maximum(_dot(h_ref[...], w1_ref[...]), 0.0)
    acc_ref[...] += _dot((u * u).astype(BF16), w2_ref[...])

    @pl.when(kf == pl.num_programs(1) - 1)
    def _():
        o_ref[...] = x1_ref[...] + m[5:6] * acc_ref[...]


def _outproj_mlp(x2, attn, mod, gln, wo, w1, w2, seq, tm, tf):
    t, d = x2.shape
    dff = w1.shape[1]
    return pl.pallas_call(
        _mlp_kernel,
        grid=(t // tm, dff // tf),
        in_specs=[
            pl.BlockSpec((tm, d), lambda i, k: (i, 0)),
            pl.BlockSpec((tm, d), lambda i, k: (i, 0)),
            pl.BlockSpec((1, 8, d), lambda i, k: ((i * tm) // seq, 0, 0)),
            pl.BlockSpec((1, d), lambda i, k: (0, 0)),
            pl.BlockSpec((d, d), lambda i, k: (0, 0)),
            pl.BlockSpec((d, tf), lambda i, k: (0, k)),
            pl.BlockSpec((tf, d), lambda i, k: (k, 0)),
        ],
        out_specs=pl.BlockSpec((tm, d), lambda i, k: (i, 0)),
        out_shape=jax.ShapeDtypeStruct((t, d), F32),
        scratch_shapes=[pltpu.VMEM((tm, d), F32), pltpu.VMEM((tm, d), BF16),
                        pltpu.VMEM((tm, d), F32)],
        compiler_params=pltpu.CompilerParams(
            dimension_semantics=("arbitrary", "arbitrary"), vmem_limit_bytes=VMEM_LIMIT),
        name="outproj_mlp",
    )(x2, attn, mod, gln, wo, w1, w2)


def _rope_lane_tables(positions):
    inv = ROPE_THETA ** (-jnp.arange(0, ROPE_DIM, 2, dtype=F32) / ROPE_DIM)
    ang = positions.astype(F32)[..., None] * inv
    cos, sin = jnp.cos(ang), jnp.sin(ang)
    rest = HEAD_DIM - ROPE_DIM
    ones = jnp.ones(cos.shape[:-1] + (rest,), F32)
    zeros = jnp.zeros(cos.shape[:-1] + (rest,), F32)
    cosh = jnp.concatenate([cos, cos, ones], axis=-1)
    sinh = jnp.concatenate([-sin, sin, zeros], axis=-1)
    reps = LANES // HEAD_DIM
    t = positions.shape[0] * positions.shape[1]
    return (jnp.tile(cosh, (1, 1, reps)).reshape(t, LANES),
            jnp.tile(sinh, (1, 1, reps)).reshape(t, LANES))


def _tile_gain(g, heads):
    return jnp.tile(g.astype(F32), heads)


def kernel(x, c, positions, ln_mix_g, ln_mlp_g, w_ada, b_ada, w_mlp_in, w_mlp_out, nsa_w_in, nsa_b_gate, nsa_q_gain, nsa_k_gain, nsa_pe_k, nsa_w_ck1, nsa_w_ck2, nsa_pe_v, nsa_w_cv1, nsa_w_cv2, nsa_w_out, diff_w_in, diff_q_gain, diff_k_gain, diff_lq1, diff_lk1, diff_lq2, diff_lk2, diff_subln_g, diff_w_out):
    b, s, d = x.shape
    depth = w_ada.shape[0]
    t = b * s
    tm = 512
    n_sel = s // SEL_BLOCK
    nq, nk = NSA_HEADS * HEAD_DIM, NSA_GROUPS * HEAD_DIM
    assert s % tm == 0 and s % 16 == 0 and n_sel <= LANES and s >= WINDOW + 128

    cosf, sinf = _rope_lane_tables(positions)
    c_pad = jnp.zeros((8, d), F32).at[:b].set(c)
    mod = _ada_mod(c_pad, w_ada, b_ada)
    mod = mod[:, :b].reshape(depth, b, 6, d)
    mod = jnp.pad(mod, ((0, 0), (0, 0), (0, 2), (0, 0)))

    n_half = s // CMP_STRIDE
    cb = jnp.arange(n_half)
    sb = jnp.arange(LANES)
    ov = ((cb[:, None] * CMP_STRIDE <= sb[None, :] * SEL_BLOCK + SEL_BLOCK - 1)
          & (cb[:, None] * CMP_STRIDE + CMP_BLOCK - 1 >= sb[None, :] * SEL_BLOCK)
          & (sb[None, :] < n_sel)).astype(BF16)
    xp = (sb[:, None] == (jnp.arange(s)[None, :] // SEL_BLOCK)).astype(BF16)

    x2 = x.reshape(t, d)
    for i in range(depth):
        j = i // 2
        gln = ln_mix_g[i].reshape(1, d)
        if i % 2 == 0:
            w = nsa_w_in[j]
            o0 = nq
            wq = w[:, :o0]
            wkc, wvc, wks, wvs, wkw, wvw = [w[:, o0 + r * nk:o0 + (r + 1) * nk] for r in range(6)]
            wgl = w[:, o0 + 6 * nk:]
            wn = jnp.concatenate([wq, wks, wkw], axis=1).astype(BF16)
            wr = jnp.concatenate([wkc, wvc, wvs, wvw], axis=1).astype(BF16)
            wg = wgl.reshape(d, NSA_HEADS, 3).transpose(0, 2, 1).reshape(d, 3 * NSA_HEADS)
            wg = jnp.pad(wg, ((0, 0), (0, LANES - 3 * NSA_HEADS))).astype(BF16)
            bg = nsa_b_gate[j].reshape(NSA_HEADS, 3).T.reshape(1, 3 * NSA_HEADS)
            bg = jnp.pad(bg, ((0, 0), (0, LANES - 3 * NSA_HEADS)))
            gain = jnp.concatenate([_tile_gain(nsa_q_gain[j], NSA_HEADS),
                                    _tile_gain(nsa_k_gain[j, 1], NSA_GROUPS),
                                    _tile_gain(nsa_k_gain[j, 2], NSA_GROUPS)]).reshape(1, -1)
            qn, qr, ks, kw, kc, vc, vs, vw, gates = _nsa_proj(
                x2, mod[i], gln, wn, wr, wg, bg, gain, cosf, sinf, s, tm)

            def halfblocks(a):
                a = a.reshape(b, n_half, CMP_STRIDE, NSA_GROUPS, HEAD_DIM)
                return a.transpose(0, 3, 1, 2, 4).reshape(b, NSA_GROUPS, n_half, CMP_STRIDE * HEAD_DIM)

            def headmajor(a):
                return a.reshape(b, s, NSA_GROUPS, HEAD_DIM).transpose(0, 2, 1, 3)

            pek = nsa_pe_k[j].reshape(2, CMP_STRIDE * HEAD_DIM)
            pev = nsa_pe_v[j].reshape(2, CMP_STRIDE * HEAD_DIM)
            kcmp, vcmp = _nsa_compress(
                halfblocks(kc), halfblocks(vc), pek, pev,
                nsa_w_ck1[j].astype(BF16), nsa_w_ck2[j].astype(BF16),
                nsa_w_cv1[j].astype(BF16), nsa_w_cv2[j].astype(BF16),
                nsa_k_gain[j, 0].reshape(1, HEAD_DIM))
            attn = _nsa_attn(qn, qr, gates, kcmp, vcmp, headmajor(ks), headmajor(vs),
                             headmajor(kw), headmajor(vw), ov, xp, tq=128, tk=256)
            wo = nsa_w_out[j]
        else:
            w = diff_w_in[j]
            dw = w.shape[1] // 3
            heads = dw // HEAD_DIM
            wn = w[:, :2 * dw].astype(BF16)
            wr = w[:, 2 * dw:].astype(BF16)
            gain = jnp.concatenate([_tile_gain(diff_q_gain[j], heads),
                                    _tile_gain(diff_k_gain[j], heads)]).reshape(1, -1)
            q, k, v = _diff_proj(x2, mod[i], gln, wn, wr, gain, cosf, sinf, s, tm)
            lamv = jnp.stack([diff_lq1[j], diff_lk1[j], diff_lq2[j], diff_lk2[j]]).astype(F32)
            lamv = jnp.pad(lamv, ((0, 4), (0, LANES - HEAD_DIM)))
            lam_init = 0.8 - 0.6 * math.exp(-0.3 * i)
            attn = _diff_attn(lamv, q, k.reshape(b, s, dw), v.reshape(b, s, dw),
                              diff_subln_g[j].reshape(1, -1), tq=256, lam_init=lam_init)
            wo = diff_w_out[j]
        x2 = _outproj_mlp(x2, attn, mod[i], ln_mlp_g[i].reshape(1, d), wo.astype(BF16),
                          w_mlp_in[i].astype(BF16), w_mlp_out[i].astype(BF16), s, tm, tf=1024)
    return x2.reshape(b, s, d)
```

```python
import functools
import math

import jax
import jax.numpy as jnp
from jax import lax
from jax.experimental import pallas as pl
from jax.experimental.pallas import tpu as pltpu

F32 = jnp.float32
BF16 = jnp.bfloat16

LANES = 128
HEAD_DIM = 64
ROPE_DIM = HEAD_DIM // 4
ROPE_THETA = 500000.0
NSA_GROUPS = 4
NSA_HPG = 4
NSA_HEADS = NSA_GROUPS * NSA_HPG
CMP_BLOCK = 32
CMP_STRIDE = 16
CMP_HIDDEN = 2 * HEAD_DIM
SEL_BLOCK = 64
SEL_TOPK = 16
WINDOW = 512
EPS = 1e-6
NEG_INF = -1e30
SEL_FORCE = 1e6
QK_SCALE = math.log2(math.e) / math.sqrt(HEAD_DIM)
ONES_ROWS = 16

VMEM_LIMIT = 48 * 1024 * 1024


def _dot(a, b):
    return jnp.dot(a, b, preferred_element_type=F32)


def _dot_nt(a, b):
    return lax.dot_general(a, b, (((1,), (1,)), ((), ())), preferred_element_type=F32)


def _sigmoid(x):
    return 1.0 / (1.0 + jnp.exp(-x))


def _split3(x):
    hi = x.astype(BF16)
    r1 = x - hi.astype(F32)
    mid = r1.astype(BF16)
    lo = (r1 - mid.astype(F32)).astype(BF16)
    return hi, mid, lo


def _mod_rmsnorm(x, g_ln, shift, scale):
    ms = jnp.mean(x * x, axis=-1, keepdims=True)
    return (x * lax.rsqrt(ms + EPS) * g_ln) * (1.0 + scale) + shift


def _head_rmsnorm(a, e_ref, et_ref, gain):
    ss = _dot((a * a).astype(BF16), e_ref[...])
    r = lax.rsqrt(ss * (1.0 / HEAD_DIM) + EPS)
    r_hi = r.astype(BF16)
    r_lo = (r - r_hi.astype(F32)).astype(BF16)
    et = et_ref[...]
    return a * (_dot(r_hi, et) + _dot(r_lo, et)) * gain


def _rope(a, cosf, sinf):
    lane = lax.broadcasted_iota(jnp.int32, (a.shape[0], LANES), 1)
    first = (lane & (HEAD_DIM - 1)) < (ROPE_DIM // 2)
    half = ROPE_DIM // 2
    outs = []
    for j in range(a.shape[1] // LANES):
        blk = a[:, j * LANES:(j + 1) * LANES]
        partner = jnp.where(first, pltpu.roll(blk, LANES - half, 1), pltpu.roll(blk, half, 1))
        outs.append(blk * cosf + partner * sinf)
    return jnp.concatenate(outs, axis=1)


def _ada_kernel(c_ref, w_ref, b_ref, o_ref):
    c = c_ref[...]
    cond = c * _sigmoid(c)
    w = w_ref[0]
    c_hi = cond.astype(BF16)
    c_lo = (cond - c_hi.astype(F32)).astype(BF16)
    w_hi = w.astype(BF16)
    w_lo = (w - w_hi.astype(F32)).astype(BF16)
    o_ref[0] = _dot(c_hi, w_hi) + _dot(c_lo, w_hi) + _dot(c_hi, w_lo) + b_ref[0]


def _ada_mod(c_pad, w_ada, b_ada):
    depth, d, n = w_ada.shape
    tn = 1536
    return pl.pallas_call(
        _ada_kernel,
        grid=(depth, n // tn),
        in_specs=[
            pl.BlockSpec((8, d), lambda l, j: (0, 0)),
            pl.BlockSpec((1, d, tn), lambda l, j: (l, 0, j)),
            pl.BlockSpec((1, 1, tn), lambda l, j: (l, 0, j)),
        ],
        out_specs=pl.BlockSpec((1, 8, tn), lambda l, j: (l, 0, j)),
        out_shape=jax.ShapeDtypeStruct((depth, 8, n), F32),
        compiler_params=pltpu.CompilerParams(
            dimension_semantics=("arbitrary", "arbitrary"), vmem_limit_bytes=VMEM_LIMIT),
        name="ada_mod",
    )(c_pad, w_ada, b_ada.reshape(depth, 1, n))


def _nsa_proj_kernel(x_ref, mod_ref, gln_ref, wn_ref, wr_ref, wg_ref, bg_ref, e_ref, et_ref,
                     gain_ref, cos_ref, sin_ref,
                     qn_ref, qr_ref, ks_ref, kw_ref, kc_ref, vc_ref, vs_ref, vw_ref, gate_ref):
    m = mod_ref[0]
    h = _mod_rmsnorm(x_ref[...], gln_ref[...], m[0:1], m[1:2]).astype(BF16)
    a = _dot(h, wn_ref[...])
    an = _head_rmsnorm(a, e_ref, et_ref, gain_ref[...])
    ar = _rope(an, cos_ref[...], sin_ref[...])
    nq = qn_ref.shape[1]
    nk = ks_ref.shape[1]
    qn_ref[...] = (an[:, :nq] * QK_SCALE).astype(BF16)
    qr_ref[...] = (ar[:, :nq] * QK_SCALE).astype(BF16)
    ks_ref[...] = ar[:, nq:nq + nk].astype(BF16)
    kw_ref[...] = ar[:, nq + nk:nq + 2 * nk].astype(BF16)
    raw = _dot(h, wr_ref[...])
    kc_ref[...] = raw[:, 0:nk]
    vc_ref[...] = raw[:, nk:2 * nk]
    vs_ref[...] = raw[:, 2 * nk:3 * nk].astype(BF16)
    vw_ref[...] = raw[:, 3 * nk:4 * nk].astype(BF16)
    gate_ref[...] = _sigmoid(_dot(h, wg_ref[...]) + bg_ref[...])


def _diff_proj_kernel(x_ref, mod_ref, gln_ref, wn_ref, wr_ref, e_ref, et_ref, gain_ref,
                      cos_ref, sin_ref, q_ref, k_ref, v_ref):
    m = mod_ref[0]
    h = _mod_rmsnorm(x_ref[...], gln_ref[...], m[0:1], m[1:2]).astype(BF16)
    a = _dot(h, wn_ref[...])
    ar = _rope(_head_rmsnorm(a, e_ref, et_ref, gain_ref[...]), cos_ref[...], sin_ref[...])
    nq = q_ref.shape[1]
    q_ref[...] = (ar[:, :nq] * QK_SCALE).astype(BF16)
    k_ref[...] = ar[:, nq:].astype(BF16)
    v_ref[...] = _dot(h, wr_ref[...]).astype(BF16)


def _row_spec(tm, n):
    return pl.BlockSpec((tm, n), lambda i: (i, 0))


def _full_spec(shape):
    return pl.BlockSpec(shape, lambda i: (0,) * len(shape))


def _head_indicator(n):
    head = jnp.arange(n)[:, None] // HEAD_DIM
    e = (head == jnp.arange(LANES)[None, :]).astype(BF16)
    return e, e.T


def _nsa_proj(x2, mod, gln, wn, wr, wg, bg, gain, cosf, sinf, seq, tm):
    t, d = x2.shape
    nn, nr = wn.shape[1], wr.shape[1]
    nq, nk = NSA_HEADS * HEAD_DIM, NSA_GROUPS * HEAD_DIM
    e, et = _head_indicator(nn)
    outs = [jax.ShapeDtypeStruct((t, nq), BF16), jax.ShapeDtypeStruct((t, nq), BF16),
            jax.ShapeDtypeStruct((t, nk), BF16), jax.ShapeDtypeStruct((t, nk), BF16),
            jax.ShapeDtypeStruct((t, nk), F32), jax.ShapeDtypeStruct((t, nk), F32),
            jax.ShapeDtypeStruct((t, nk), BF16), jax.ShapeDtypeStruct((t, nk), BF16),
            jax.ShapeDtypeStruct((t, LANES), F32)]
    return pl.pallas_call(
        _nsa_proj_kernel,
        grid=(t // tm,),
        in_specs=[
            _row_spec(tm, d),
            pl.BlockSpec((1, 8, d), lambda i: ((i * tm) // seq, 0, 0)),
            _full_spec((1, d)), _full_spec((d, nn)), _full_spec((d, nr)), _full_spec((d, LANES)),
            _full_spec((1, LANES)), _full_spec((nn, LANES)), _full_spec((LANES, nn)),
            _full_spec((1, nn)), _row_spec(tm, LANES), _row_spec(tm, LANES),
        ],
        out_specs=[_row_spec(tm, o.shape[1]) for o in outs],
        out_shape=outs,
        compiler_params=pltpu.CompilerParams(
            dimension_semantics=("arbitrary",), vmem_limit_bytes=VMEM_LIMIT),
        name="nsa_proj",
    )(x2, mod, gln, wn, wr, wg, bg, e, et, gain, cosf, sinf)


def _diff_proj(x2, mod, gln, wn, wr, gain, cosf, sinf, seq, tm):
    t, d = x2.shape
    nn, nr = wn.shape[1], wr.shape[1]
    e, et = _head_indicator(nn)
    outs = [jax.ShapeDtypeStruct((t, nn // 2), BF16), jax.ShapeDtypeStruct((t, nn // 2), BF16),
            jax.ShapeDtypeStruct((t, nr), BF16)]
    return pl.pallas_call(
        _diff_proj_kernel,
        grid=(t // tm,),
        in_specs=[
            _row_spec(tm, d),
            pl.BlockSpec((1, 8, d), lambda i: ((i * tm) // seq, 0, 0)),
            _full_spec((1, d)), _full_spec((d, nn)), _full_spec((d, nr)),
            _full_spec((nn, LANES)), _full_spec((LANES, nn)), _full_spec((1, nn)),
            _row_spec(tm, LANES), _row_spec(tm, LANES),
        ],
        out_specs=[_row_spec(tm, o.shape[1]) for o in outs],
        out_shape=outs,
        compiler_params=pltpu.CompilerParams(
            dimension_semantics=("arbitrary",), vmem_limit_bytes=VMEM_LIMIT),
        name="diff_proj",
    )(x2, mod, gln, wn, wr, e, et, gain, cosf, sinf)


def _compress_one(x, pe_ref, w1_ref, w2_ref):
    half = x.shape[1]
    za = (x + pe_ref[0:1, :]).astype(BF16)
    zb = (x + pe_ref[1:2, :]).astype(BF16)
    a = _dot(za, w1_ref[0:half, :])
    b = _dot(zb, w1_ref[half:2 * half, :])
    pre = a + pltpu.roll(b, x.shape[0] - 1, 0)
    hid = pre * _sigmoid(pre)
    return _dot(hid.astype(BF16), w2_ref[...])


def _compress_kernel(xk_ref, xv_ref, pek_ref, pev_ref, wk1_ref, wk2_ref, wv1_ref, wv2_ref,
                     kg_ref, kc_ref, vc_ref):
    k = _compress_one(xk_ref[0, 0], pek_ref, wk1_ref, wk2_ref)
    ms = jnp.mean(k * k, axis=-1, keepdims=True)
    kc_ref[0, 0] = (k * lax.rsqrt(ms + EPS) * kg_ref[...]).astype(BF16)
    vc_ref[0, 0] = _compress_one(xv_ref[0, 0], pev_ref, wv1_ref, wv2_ref).astype(BF16)


def _nsa_compress(xk, xv, pek, pev, wk1, wk2, wv1, wv2, kg0):
    b, g, nh, w = xk.shape
    blk = pl.BlockSpec((1, 1, nh, w), lambda i, j: (i, j, 0, 0))
    full = lambda s: pl.BlockSpec(s, lambda i, j: (0,) * len(s))
    out = jax.ShapeDtypeStruct((b, g, nh, HEAD_DIM), BF16)
    oblk = pl.BlockSpec((1, 1, nh, HEAD_DIM), lambda i, j: (i, j, 0, 0))
    return pl.pallas_call(
        _compress_kernel,
        grid=(b, g),
        in_specs=[blk, blk, full(pek.shape), full(pev.shape), full(wk1.shape), full(wk2.shape),
                  full(wv1.shape), full(wv2.shape), full(kg0.shape)],
        out_specs=[oblk, oblk],
        out_shape=[out, out],
        compiler_params=pltpu.CompilerParams(
            dimension_semantics=("arbitrary", "arbitrary"), vmem_limit_bytes=VMEM_LIMIT),
        name="nsa_compress",
    )(xk, xv, pek, pev, wk1, wk2, wv1, wv2, kg0)


def _half_masks(rows):
    lane = lax.broadcasted_iota(jnp.int32, (rows, LANES), 1)
    return lane < HEAD_DIM, lane >= HEAD_DIM


def _flash_update(st, vt, m_ref, acc_ref):
    m_old = m_ref[...]
    m_new = jnp.maximum(m_old, jnp.max(st, axis=0, keepdims=True))
    p = jnp.exp2(st - m_new).astype(BF16)
    acc_ref[...] = jnp.exp2(m_old - m_new) * acc_ref[...] + _dot(vt, p)
    m_ref[...] = m_new


def _flash_pipeline(n_full, tk, scores, consume, s0_ref, s1_ref, m_ref, acc_ref):
    m_ref[...] = jnp.full(m_ref.shape, NEG_INF, F32)
    acc_ref[...] = jnp.zeros(acc_ref.shape, F32)
    scores(s0_ref, 0)

    def pair(jj, carry):
        k0 = pl.multiple_of(jj * (2 * tk), 2 * tk)
        scores(s1_ref, k0 + tk)
        consume(s0_ref, k0, False)
        scores(s0_ref, k0 + 2 * tk)
        consume(s1_ref, k0 + tk, False)
        return carry

    n_pairs = lax.shift_right_logical(n_full, 1)
    lax.fori_loop(0, n_pairs, pair, 0)
    k0 = pl.multiple_of(n_pairs * (2 * tk), 2 * tk)
    odd = (n_full & 1) == 1

    @pl.when(odd)
    def _():
        scores(s1_ref, k0 + tk)
        consume(s0_ref, k0, False)
        consume(s1_ref, k0 + tk, True)

    @pl.when(jnp.logical_not(odd))
    def _():
        consume(s0_ref, k0, True)


def _nsa_attn_kernel(qn_ref, qr_ref, gate_ref, kc_ref, vct_ref, ks_ref, vst_ref, kw_ref, vwt_ref,
                     ovt_ref, o_ref, gt_ref, s0_ref, s1_ref, m_ref, acc_ref, *, tq, tk, n_sel):
    g = pl.program_id(1)
    q0 = pl.program_id(2) * tq
    nqs = NSA_HPG * tq
    ncb = kc_ref.shape[2]
    lo, hi = _half_masks(tq)
    tq_row = q0 + lax.broadcasted_iota(jnp.int32, (1, tq), 1)
    t4 = q0 + (lax.broadcasted_iota(jnp.int32, (1, nqs), 1) & (tq - 1))

    def stack_heads(ref):
        parts = []
        for j in range(NSA_HPG):
            pair = ref[:, (j // 2) * LANES:(j // 2 + 1) * LANES]
            parts.append(jnp.where(lo if j % 2 == 0 else hi, pair, jnp.zeros_like(pair)))
        return jnp.concatenate(parts, axis=0)

    qn_s = stack_heads(qn_ref)
    qr_s = stack_heads(qr_ref)

    st = _dot_nt(kc_ref[0, 0], qn_s)
    cend = lax.broadcasted_iota(jnp.int32, (ncb, nqs), 0) * CMP_STRIDE + (CMP_BLOCK - 1)
    cmask = cend <= t4
    st = jnp.where(cmask, st, NEG_INF)
    p = jnp.where(cmask, jnp.exp2(st - jnp.max(st, axis=0, keepdims=True)), 0.0)
    l = jnp.sum(p, axis=0, keepdims=True)
    p = p * (1.0 / jnp.where(l > 0.0, l, 1.0))
    o_cmp = _dot(vct_ref[0, 0], p.astype(BF16))
    pc_sum = p[:, 0:tq]
    for j in range(1, NSA_HPG):
        pc_sum = pc_sum + p[:, j * tq:(j + 1) * tq]

    ovt = ovt_ref[...]
    imp = sum(_dot(ovt, part) for part in _split3(pc_sum))
    blk = lax.broadcasted_iota(jnp.int32, (n_sel, tq), 0)
    bt = lax.shift_right_arithmetic(tq_row, int(math.log2(SEL_BLOCK)))
    forced = (blk == 0) | (blk == bt) | (blk == bt - 1)
    val = jnp.where(forced, SEL_FORCE, jnp.where(blk <= bt, imp[0:n_sel, :], -1.0))
    rank = jnp.zeros((n_sel, tq), F32)
    for jp in range(n_sel):
        row = val[jp:jp + 1, :]
        beats = (row > val) | ((row == val) & (blk > jp))
        rank = rank + jnp.where(beats, 1.0, 0.0)
    selb = jnp.where(rank < float(min(SEL_TOPK, n_sel)), 0.0, NEG_INF)
    selb = jnp.concatenate([selb, jnp.zeros((LANES - n_sel, tq), F32)], axis=0).T
    selb = selb.astype(BF16)
    q_aug = jnp.concatenate([qr_s, jnp.concatenate([selb] * NSA_HPG, axis=0)], axis=1)

    def scores(s_ref, k0):
        s_ref[...] = _dot_nt(ks_ref[0, 0, pl.ds(k0, tk), :], q_aug)

    def consume(s_ref, k0, causal):
        st = s_ref[...]
        if causal:
            kpos = k0 + lax.broadcasted_iota(jnp.int32, (tk, nqs), 0)
            st = jnp.where(kpos <= t4, st, NEG_INF)
        _flash_update(st, vst_ref[0, 0, :, pl.ds(k0, tk)], m_ref, acc_ref)

    _flash_pipeline(q0 // tk, tk, scores, consume, s0_ref, s1_ref, m_ref, acc_ref)
    acc = acc_ref[...]
    o_sel = acc[0:HEAD_DIM] * (1.0 / acc[HEAD_DIM:HEAD_DIM + 1])

    nw = WINDOW + tq
    start = pl.multiple_of(jnp.maximum(q0 - WINDOW, 0), tq)
    st = _dot_nt(kw_ref[0, 0, pl.ds(start, nw), :], qr_s)
    wpos = start + lax.broadcasted_iota(jnp.int32, (nw, nqs), 0)
    st = jnp.where((wpos <= t4) & (wpos > t4 - WINDOW), st, NEG_INF)
    p = jnp.exp2(st - jnp.max(st, axis=0, keepdims=True))
    o_win = _dot(vwt_ref[0, 0, :, pl.ds(start, nw)], p.astype(BF16)) * (
        1.0 / jnp.sum(p, axis=0, keepdims=True))

    gt_ref[...] = gate_ref[...].T

    def gate_rows(branch):
        base = branch * NSA_HEADS + g * NSA_HPG
        return jnp.concatenate([gt_ref[pl.ds(base + j, 1), :] for j in range(NSA_HPG)], axis=1)

    o = gate_rows(0) * o_cmp + gate_rows(1) * o_sel + gate_rows(2) * o_win
    o = jnp.concatenate([o[:, j * tq:(j + 1) * tq] for j in range(NSA_HPG)], axis=0)
    o_ref[...] = o.T.astype(BF16)


def _nsa_attn(qn, qr, gates, kc_dup, vct, ks_aug, vst, kw_dup, vwt, ovt, tq, tk):
    b, g, s, _ = ks_aug.shape
    t = qn.shape[0]
    gw = NSA_HPG * HEAD_DIM
    nq = s // tq
    ncb = kc_dup.shape[2]
    qspec = pl.BlockSpec((tq, gw), lambda bi, gi, i: (bi * nq + i, gi))
    per_bg = lambda r, c: pl.BlockSpec((1, 1, r, c), lambda bi, gi, i: (bi, gi, 0, 0))
    return pl.pallas_call(
        functools.partial(_nsa_attn_kernel, tq=tq, tk=tk, n_sel=s // SEL_BLOCK),
        grid=(b, g, nq),
        in_specs=[qspec, qspec,
                  pl.BlockSpec((tq, LANES), lambda bi, gi, i: (bi * nq + i, 0)),
                  per_bg(ncb, LANES), per_bg(HEAD_DIM, ncb),
                  per_bg(s, 2 * LANES), per_bg(vst.shape[2], s),
                  per_bg(s, LANES), per_bg(HEAD_DIM, s),
                  pl.BlockSpec(ovt.shape, lambda bi, gi, i: (0, 0))],
        out_specs=qspec,
        out_shape=jax.ShapeDtypeStruct((t, g * gw), BF16),
        scratch_shapes=[pltpu.VMEM((LANES, tq), F32),
                        pltpu.VMEM((tk, NSA_HPG * tq), F32),
                        pltpu.VMEM((tk, NSA_HPG * tq), F32),
                        pltpu.VMEM((1, NSA_HPG * tq), F32),
                        pltpu.VMEM((vst.shape[2], NSA_HPG * tq), F32)],
        compiler_params=pltpu.CompilerParams(
            dimension_semantics=("arbitrary", "arbitrary", "arbitrary"),
            vmem_limit_bytes=VMEM_LIMIT),
        name="nsa_attn",
    )(qn, qr, gates, kc_dup, vct, ks_aug, vst, kw_dup, vwt, ovt)


def _diff_attn_kernel(lamv_ref, q_ref, k_ref, vt_ref, sg_ref, o_ref, s0_ref, s1_ref, m_ref, acc_ref,
                      *, tq, hs, lam_init):
    vw = vt_ref.shape[1] // hs
    lv = lamv_ref[...]
    lam = (jnp.exp(jnp.sum(lv[0:1] * lv[1:2], axis=-1, keepdims=True))
           - jnp.exp(jnp.sum(lv[2:3] * lv[3:4], axis=-1, keepdims=True)) + lam_init)
    hw = 2 * HEAD_DIM
    lo, hi = _half_masks(tq)
    q_s = []
    for u in range(hs):
        q = q_ref[:, u * hw:(u + 1) * hw]
        zero = jnp.zeros_like(q)
        q_s.append(jnp.concatenate([jnp.where(lo, q, zero), jnp.where(hi, q, zero)], axis=0))

    def scores(s_ref, k0):
        for u in range(hs):
            s_ref[u] = _dot_nt(k_ref[0, pl.ds(k0, tq), u * hw:(u + 1) * hw], q_s[u])

    def consume(s_ref, k0, causal):
        for u in range(hs):
            st = s_ref[u]
            if causal:
                kpos = lax.broadcasted_iota(jnp.int32, st.shape, 0)
                qpos = lax.broadcasted_iota(jnp.int32, st.shape, 1) & (tq - 1)
                st = jnp.where(kpos <= qpos, st, NEG_INF)
            _flash_update(st, vt_ref[0, u * vw:(u + 1) * vw, pl.ds(k0, tq)], m_ref.at[u],
                          acc_ref.at[u])

    _flash_pipeline(pl.program_id(2), tq, scores, consume, s0_ref, s1_ref, m_ref, acc_ref)

    for u in range(hs):
        acc = acc_ref[u]
        o = acc[0:hw] * (1.0 / acc[hw:hw + 1])
        o = (o[:, 0:tq] - lam * o[:, tq:2 * tq]).T
        ms = jnp.mean(o * o, axis=-1, keepdims=True)
        o_ref[:, u * hw:(u + 1) * hw] = (
            (o * lax.rsqrt(ms + EPS) * sg_ref[...]) * (1.0 - lam_init)).astype(BF16)


def _diff_attn(lamv, q, k, vt, sg, tq, hs, lam_init):
    b, s, w = k.shape
    hw = 2 * HEAD_DIM
    heads = w // hw
    vw = vt.shape[1] // heads
    nq = s // tq
    full = lambda shp: pl.BlockSpec(shp, lambda bi, hi, i: (0,) * len(shp))
    qspec = pl.BlockSpec((tq, hs * hw), lambda bi, hi, i: (bi * nq + i, hi))
    return pl.pallas_call(
        functools.partial(_diff_attn_kernel, tq=tq, hs=hs, lam_init=lam_init),
        grid=(b, heads // hs, nq),
        in_specs=[full(lamv.shape), qspec,
                  pl.BlockSpec((1, s, hs * hw), lambda bi, hi, i: (bi, 0, hi)),
                  pl.BlockSpec((1, hs * vw, s), lambda bi, hi, i: (bi, hi, 0)),
                  full(sg.shape)],
        out_specs=qspec,
        out_shape=jax.ShapeDtypeStruct(q.shape, BF16),
        scratch_shapes=[pltpu.VMEM((hs, tq, 2 * tq), F32), pltpu.VMEM((hs, tq, 2 * tq), F32),
                        pltpu.VMEM((hs, 1, 2 * tq), F32), pltpu.VMEM((hs, vw, 2 * tq), F32)],
        compiler_params=pltpu.CompilerParams(
            dimension_semantics=("arbitrary", "arbitrary", "arbitrary"),
            vmem_limit_bytes=VMEM_LIMIT),
        name="diff_attn",
    )(lamv, q, k, vt, sg)


def _mlp_kernel(x_ref, a_ref, mod_ref, gln_ref, wo_ref, w1_ref, w2_ref, o_ref,
                x1_ref, h_ref, acc_ref):
    kf = pl.program_id(1)
    m = mod_ref[0]

    @pl.when(kf == 0)
    def _():
        x1 = x_ref[...] + m[2:3] * _dot(a_ref[...], wo_ref[...])
        x1_ref[...] = x1
        h_ref[...] = _mod_rmsnorm(x1, gln_ref[...], m[3:4], m[4:5]).astype(BF16)
        acc_ref[...] = jnp.zeros(acc_ref.shape, F32)

    u = jnp.maximum(_dot(h_ref[...], w1_ref[...]), 0.0)
    acc_ref[...] += _dot((u * u).astype(BF16), w2_ref[...])

    @pl.when(kf == pl.num_programs(1) - 1)
    def _():
        o_ref[...] = x1_ref[...] + m[5:6] * acc_ref[...]


def _outproj_mlp(x2, attn, mod, gln, wo, w1, w2, seq, tm, tf):
    t, d = x2.shape
    dff = w1.shape[1]
    return pl.pallas_call(
        _mlp_kernel,
        grid=(t // tm, dff // tf),
        in_specs=[
            pl.BlockSpec((tm, d), lambda i, k: (i, 0)),
            pl.BlockSpec((tm, d), lambda i, k: (i, 0)),
            pl.BlockSpec((1, 8, d), lambda i, k: ((i * tm) // seq, 0, 0)),
            pl.BlockSpec((1, d), lambda i, k: (0, 0)),
            pl.BlockSpec((d, d), lambda i, k: (0, 0)),
            pl.BlockSpec((d, tf), lambda i, k: (0, k)),
            pl.BlockSpec((tf, d), lambda i, k: (k, 0)),
        ],
        out_specs=pl.BlockSpec((tm, d), lambda i, k: (i, 0)),
        out_shape=jax.ShapeDtypeStruct((t, d), F32),
        scratch_shapes=[pltpu.VMEM((tm, d), F32), pltpu.VMEM((tm, d), BF16),
                        pltpu.VMEM((tm, d), F32)],
        compiler_params=pltpu.CompilerParams(
            dimension_semantics=("arbitrary", "arbitrary"), vmem_limit_bytes=VMEM_LIMIT),
        name="outproj_mlp",
    )(x2, attn, mod, gln, wo, w1, w2)


def _rope_lane_tables(positions):
    inv = ROPE_THETA ** (-jnp.arange(0, ROPE_DIM, 2, dtype=F32) / ROPE_DIM)
    ang = positions.astype(F32)[..., None] * inv
    cos, sin = jnp.cos(ang), jnp.sin(ang)
    rest = HEAD_DIM - ROPE_DIM
    ones = jnp.ones(cos.shape[:-1] + (rest,), F32)
    zeros = jnp.zeros(cos.shape[:-1] + (rest,), F32)
    cosh = jnp.concatenate([cos, cos, ones], axis=-1)
    sinh = jnp.concatenate([-sin, sin, zeros], axis=-1)
    reps = LANES // HEAD_DIM
    t = positions.shape[0] * positions.shape[1]
    return (jnp.tile(cosh, (1, 1, reps)).reshape(t, LANES),
            jnp.tile(sinh, (1, 1, reps)).reshape(t, LANES))


def _tile_gain(g, heads):
    return jnp.tile(g.astype(F32), heads)


def kernel(x, c, positions, ln_mix_g, ln_mlp_g, w_ada, b_ada, w_mlp_in, w_mlp_out, nsa_w_in, nsa_b_gate, nsa_q_gain, nsa_k_gain, nsa_pe_k, nsa_w_ck1, nsa_w_ck2, nsa_pe_v, nsa_w_cv1, nsa_w_cv2, nsa_w_out, diff_w_in, diff_q_gain, diff_k_gain, diff_lq1, diff_lk1, diff_lq2, diff_lk2, diff_subln_g, diff_w_out):
    b, s, d = x.shape
    depth = w_ada.shape[0]
    t = b * s
    tm = 512
    n_sel = s // SEL_BLOCK
    nq, nk = NSA_HEADS * HEAD_DIM, NSA_GROUPS * HEAD_DIM
    assert s % tm == 0 and s % 16 == 0 and n_sel <= LANES and s >= WINDOW + 128

    cosf, sinf = _rope_lane_tables(positions)
    c_pad = jnp.zeros((8, d), F32).at[:b].set(c)
    mod = _ada_mod(c_pad, w_ada, b_ada)
    mod = mod[:, :b].reshape(depth, b, 6, d)
    mod = jnp.pad(mod, ((0, 0), (0, 0), (0, 2), (0, 0)))

    n_half = s // CMP_STRIDE
    cb = jnp.arange(n_half)
    sb = jnp.arange(LANES)
    ovt = ((cb[None, :] * CMP_STRIDE <= sb[:, None] * SEL_BLOCK + SEL_BLOCK - 1)
           & (cb[None, :] * CMP_STRIDE + CMP_BLOCK - 1 >= sb[:, None] * SEL_BLOCK)
           & (sb[:, None] < n_sel)).astype(BF16)
    key_blk = ((jnp.arange(s)[:, None] // SEL_BLOCK) == sb[None, :]).astype(BF16)

    x2 = x.reshape(t, d)
    for i in range(depth):
        j = i // 2
        gln = ln_mix_g[i].reshape(1, d)
        if i % 2 == 0:
            w = nsa_w_in[j]
            o0 = nq
            wq = w[:, :o0]
            wkc, wvc, wks, wvs, wkw, wvw = [w[:, o0 + r * nk:o0 + (r + 1) * nk] for r in range(6)]
            wgl = w[:, o0 + 6 * nk:]
            wn = jnp.concatenate([wq, wks, wkw], axis=1).astype(BF16)
            wr = jnp.concatenate([wkc, wvc, wvs, wvw], axis=1).astype(BF16)
            wg = wgl.reshape(d, NSA_HEADS, 3).transpose(0, 2, 1).reshape(d, 3 * NSA_HEADS)
            wg = jnp.pad(wg, ((0, 0), (0, LANES - 3 * NSA_HEADS))).astype(BF16)
            bg = nsa_b_gate[j].reshape(NSA_HEADS, 3).T.reshape(1, 3 * NSA_HEADS)
            bg = jnp.pad(bg, ((0, 0), (0, LANES - 3 * NSA_HEADS)))
            gain = jnp.concatenate([_tile_gain(nsa_q_gain[j], NSA_HEADS),
                                    _tile_gain(nsa_k_gain[j, 1], NSA_GROUPS),
                                    _tile_gain(nsa_k_gain[j, 2], NSA_GROUPS)]).reshape(1, -1)
            qn, qr, ks, kw, kc, vc, vs, vw, gates = _nsa_proj(
                x2, mod[i], gln, wn, wr, wg, bg, gain, cosf, sinf, s, tm)

            def halfblocks(a):
                a = a.reshape(b, n_half, CMP_STRIDE, NSA_GROUPS, HEAD_DIM)
                return a.transpose(0, 3, 1, 2, 4).reshape(b, NSA_GROUPS, n_half, CMP_STRIDE * HEAD_DIM)

            def headmajor(a):
                return a.reshape(b, s, NSA_GROUPS, HEAD_DIM).transpose(0, 2, 1, 3)

            def headmajor_t(a):
                return a.reshape(b, s, NSA_GROUPS, HEAD_DIM).transpose(0, 2, 3, 1)

            pek = nsa_pe_k[j].reshape(2, CMP_STRIDE * HEAD_DIM)
            pev = nsa_pe_v[j].reshape(2, CMP_STRIDE * HEAD_DIM)
            kcmp, vcmp = _nsa_compress(
                halfblocks(kc), halfblocks(vc), pek, pev,
                nsa_w_ck1[j].astype(BF16), nsa_w_ck2[j].astype(BF16),
                nsa_w_cv1[j].astype(BF16), nsa_w_cv2[j].astype(BF16),
                nsa_k_gain[j, 0].reshape(1, HEAD_DIM))
            ksh, kwh = headmajor(ks), headmajor(kw)
            ks_aug = jnp.concatenate(
                [ksh, ksh, jnp.broadcast_to(key_blk, (b, NSA_GROUPS, s, LANES))], axis=-1)
            vst = jnp.concatenate(
                [headmajor_t(vs), jnp.ones((b, NSA_GROUPS, ONES_ROWS, s), BF16)], axis=2)
            attn = _nsa_attn(qn, qr, gates, jnp.concatenate([kcmp, kcmp], axis=-1),
                             vcmp.transpose(0, 1, 3, 2), ks_aug, vst,
                             jnp.concatenate([kwh, kwh], axis=-1), headmajor_t(vw), ovt,
                             tq=256, tk=256)
            wo = nsa_w_out[j]
        else:
            w = diff_w_in[j]
            dw = w.shape[1] // 3
            heads = dw // HEAD_DIM
            wn = w[:, :2 * dw].astype(BF16)
            wr = w[:, 2 * dw:].astype(BF16)
            gain = jnp.concatenate([_tile_gain(diff_q_gain[j], heads),
                                    _tile_gain(diff_k_gain[j], heads)]).reshape(1, -1)
            q, k, v = _diff_proj(x2, mod[i], gln, wn, wr, gain, cosf, sinf, s, tm)
            lamv = jnp.stack([diff_lq1[j], diff_lk1[j], diff_lq2[j], diff_lk2[j]]).astype(F32)
            lamv = jnp.pad(lamv, ((0, 4), (0, LANES - HEAD_DIM)))
            lam_init = 0.8 - 0.6 * math.exp(-0.3 * i)
            nh = dw // (2 * HEAD_DIM)
            vt = v.reshape(b, s, nh, 2 * HEAD_DIM).transpose(0, 2, 3, 1)
            vt = jnp.concatenate([vt, jnp.ones((b, nh, ONES_ROWS, s), BF16)], axis=2)
            attn = _diff_attn(lamv, q, k.reshape(b, s, dw), vt.reshape(b, -1, s),
                              diff_subln_g[j].reshape(1, -1), tq=256, hs=2, lam_init=lam_init)
            wo = diff_w_out[j]
        x2 = _outproj_mlp(x2, attn, mod[i], ln_mlp_g[i].reshape(1, d), wo.astype(BF16),
                          w_mlp_in[i].astype(BF16), w_mlp_out[i].astype(BF16), s, tm, tf=1024)
    return x2.reshape(b, s, d)
```

```python
import functools
import math

import jax
import jax.numpy as jnp
from jax import lax
from jax.experimental import pallas as pl
from jax.experimental.pallas import tpu as pltpu

F32 = jnp.float32
BF16 = jnp.bfloat16

LANES = 128
HEAD_DIM = 64
ROPE_DIM = HEAD_DIM // 4
ROPE_THETA = 500000.0
NSA_GROUPS = 4
NSA_HPG = 4
NSA_HEADS = NSA_GROUPS * NSA_HPG
CMP_BLOCK = 32
CMP_STRIDE = 16
CMP_HIDDEN = 2 * HEAD_DIM
SEL_BLOCK = 64
SEL_TOPK = 16
WINDOW = 512
EPS = 1e-6
NEG_INF = -1e30
SEL_FORCE = 1e6
QK_SCALE = math.log2(math.e) / math.sqrt(HEAD_DIM)
ONES_ROWS = 16

VMEM_LIMIT = 48 * 1024 * 1024
TM_PROJ = 512
TM_MLP, TF_MLP = 512, 1024
TQ_NSA, TK_NSA, GS_NSA = 256, 256, 2
TQ_DIFF, HS_DIFF = 256, 4


def _dot(a, b):
    return jnp.dot(a, b, preferred_element_type=F32)


def _dot_nt(a, b):
    return lax.dot_general(a, b, (((1,), (1,)), ((), ())), preferred_element_type=F32)


def _sigmoid(x):
    return 1.0 / (1.0 + jnp.exp(-x))


def _split3(x):
    hi = x.astype(BF16)
    r1 = x - hi.astype(F32)
    mid = r1.astype(BF16)
    lo = (r1 - mid.astype(F32)).astype(BF16)
    return hi, mid, lo


def _mod_rmsnorm(x, g_ln, shift, scale):
    ms = jnp.mean(x * x, axis=-1, keepdims=True)
    return (x * lax.rsqrt(ms + EPS) * g_ln) * (1.0 + scale) + shift


def _head_rmsnorm(a, e_ref, et_ref, gain):
    ss = _dot((a * a).astype(BF16), e_ref[...])
    r = lax.rsqrt(ss * (1.0 / HEAD_DIM) + EPS)
    r_hi = r.astype(BF16)
    r_lo = (r - r_hi.astype(F32)).astype(BF16)
    et = et_ref[...]
    return a * (_dot(r_hi, et) + _dot(r_lo, et)) * gain


def _rope(a, cosf, sinf):
    lane = lax.broadcasted_iota(jnp.int32, (a.shape[0], LANES), 1)
    first = (lane & (HEAD_DIM - 1)) < (ROPE_DIM // 2)
    half = ROPE_DIM // 2
    outs = []
    for j in range(a.shape[1] // LANES):
        blk = a[:, j * LANES:(j + 1) * LANES]
        partner = jnp.where(first, pltpu.roll(blk, LANES - half, 1), pltpu.roll(blk, half, 1))
        outs.append(blk * cosf + partner * sinf)
    return jnp.concatenate(outs, axis=1)


def _ada_kernel(c_ref, w_ref, b_ref, o_ref):
    c = c_ref[...]
    cond = c * _sigmoid(c)
    w = w_ref[0]
    c_hi = cond.astype(BF16)
    c_lo = (cond - c_hi.astype(F32)).astype(BF16)
    w_hi = w.astype(BF16)
    w_lo = (w - w_hi.astype(F32)).astype(BF16)
    o_ref[0] = _dot(c_hi, w_hi) + _dot(c_lo, w_hi) + _dot(c_hi, w_lo) + b_ref[0]


def _ada_mod(c_pad, w_ada, b_ada):
    depth, d, n = w_ada.shape
    tn = 1536
    return pl.pallas_call(
        _ada_kernel,
        grid=(depth, n // tn),
        in_specs=[
            pl.BlockSpec((8, d), lambda l, j: (0, 0)),
            pl.BlockSpec((1, d, tn), lambda l, j: (l, 0, j)),
            pl.BlockSpec((1, 1, tn), lambda l, j: (l, 0, j)),
        ],
        out_specs=pl.BlockSpec((1, 8, tn), lambda l, j: (l, 0, j)),
        out_shape=jax.ShapeDtypeStruct((depth, 8, n), F32),
        compiler_params=pltpu.CompilerParams(
            dimension_semantics=("arbitrary", "arbitrary"), vmem_limit_bytes=VMEM_LIMIT),
        name="ada_mod",
    )(c_pad, w_ada, b_ada.reshape(depth, 1, n))


def _store_vt(vt_ref, v, dv):
    step = max(dv, LANES)
    for c in range(v.shape[1] // step):
        vt = v[:, c * step:(c + 1) * step].T
        for r in range(step // dv):
            j = c * (step // dv) + r
            vt_ref[0, j, 0:dv, :] = vt[r * dv:(r + 1) * dv, :].astype(BF16)
            if vt_ref.shape[2] > dv:
                vt_ref[0, j, dv:, :] = jnp.ones((vt_ref.shape[2] - dv, v.shape[0]), BF16)


def _nsa_proj_kernel(x_ref, mod_ref, gln_ref, wn_ref, wr_ref, wg_ref, bg_ref, e_ref, et_ref,
                     gain_ref, cos_ref, sin_ref,
                     qn_ref, qr_ref, ks_ref, kw_ref, kc_ref, vc_ref, vst_ref, vwt_ref, gate_ref):
    tm = x_ref.shape[0]
    m = mod_ref[0]
    h = _mod_rmsnorm(x_ref[...], gln_ref[...], m[0:1], m[1:2]).astype(BF16)
    a = _dot(h, wn_ref[...])
    an = _head_rmsnorm(a, e_ref, et_ref, gain_ref[...])
    ar = _rope(an, cos_ref[...], sin_ref[...])
    nq = qn_ref.shape[1]
    kd = 2 * HEAD_DIM
    qn_ref[...] = (an[:, :nq] * QK_SCALE).astype(BF16)
    qr_ref[...] = (ar[:, :nq] * QK_SCALE).astype(BF16)
    pos = pl.program_id(1) * tm + lax.broadcasted_iota(jnp.int32, (tm, LANES), 0)
    lane = lax.broadcasted_iota(jnp.int32, (tm, LANES), 1)
    blk_onehot = jnp.where(
        lax.shift_right_logical(pos, int(math.log2(SEL_BLOCK))) == lane, 1.0, 0.0).astype(BF16)
    for g in range(NSA_GROUPS):
        ks_ref[0, g, :, 0:kd] = ar[:, nq + g * kd:nq + (g + 1) * kd].astype(BF16)
        ks_ref[0, g, :, kd:2 * kd] = blk_onehot
        off = nq + NSA_GROUPS * kd
        kw_ref[0, g] = ar[:, off + g * kd:off + (g + 1) * kd].astype(BF16)
    raw = _dot(h, wr_ref[...])
    nk = kc_ref.shape[1]
    kc_ref[...] = raw[:, 0:nk]
    vc_ref[...] = raw[:, nk:2 * nk]
    _store_vt(vst_ref, raw[:, 2 * nk:3 * nk], HEAD_DIM)
    _store_vt(vwt_ref, raw[:, 3 * nk:4 * nk], HEAD_DIM)
    gate_ref[...] = _sigmoid(_dot(h, wg_ref[...]) + bg_ref[...])


def _diff_proj_kernel(x_ref, mod_ref, gln_ref, wn_ref, wr_ref, e_ref, et_ref, gain_ref,
                      cos_ref, sin_ref, q_ref, k_ref, vt_ref):
    m = mod_ref[0]
    h = _mod_rmsnorm(x_ref[...], gln_ref[...], m[0:1], m[1:2]).astype(BF16)
    a = _dot(h, wn_ref[...])
    ar = _rope(_head_rmsnorm(a, e_ref, et_ref, gain_ref[...]), cos_ref[...], sin_ref[...])
    nq = q_ref.shape[1]
    q_ref[...] = (ar[:, :nq] * QK_SCALE).astype(BF16)
    k_ref[0] = ar[:, nq:].astype(BF16)
    _store_vt(vt_ref, _dot(h, wr_ref[...]), 2 * HEAD_DIM)


def _head_indicator(n):
    head = jnp.arange(n)[:, None] // HEAD_DIM
    e = (head == jnp.arange(LANES)[None, :]).astype(BF16)
    return e, e.T


def _proj_call(kernel, name, batch, seq, row_inputs, full_inputs, outs):
    tm = TM_PROJ
    nt = seq // tm
    rows = lambda n: pl.BlockSpec((tm, n), lambda b, i: (b * nt + i, 0))
    full = lambda shp: pl.BlockSpec(shp, lambda b, i: (0,) * len(shp))
    x2, cosf, sinf = row_inputs
    mod = full_inputs[0]
    in_specs = ([rows(x2.shape[1]), pl.BlockSpec((1,) + mod.shape[1:], lambda b, i: (b, 0, 0))]
                + [full(a.shape) for a in full_inputs[1:]]
                + [rows(cosf.shape[1]), rows(sinf.shape[1])])
    return pl.pallas_call(
        kernel,
        grid=(batch, nt),
        in_specs=in_specs,
        out_specs=[pl.BlockSpec(blk, imap) for (_, _, blk, imap) in outs],
        out_shape=[jax.ShapeDtypeStruct(shp, dt) for (shp, dt, _, _) in outs],
        compiler_params=pltpu.CompilerParams(
            dimension_semantics=("arbitrary", "arbitrary"), vmem_limit_bytes=VMEM_LIMIT),
        name=name,
    )(x2, *full_inputs, cosf, sinf)


def _nsa_proj(x2, mod, gln, wn, wr, wg, bg, gain, cosf, sinf, batch, seq):
    t = x2.shape[0]
    tm = TM_PROJ
    nt = seq // tm
    nq, nk, g = NSA_HEADS * HEAD_DIM, NSA_GROUPS * HEAD_DIM, NSA_GROUPS
    e, et = _head_indicator(wn.shape[1])
    rows = lambda n, dt: ((t, n), dt, (tm, n), lambda b, i: (b * nt + i, 0))
    keys = lambda n: ((batch, g, seq, n), BF16, (1, g, tm, n), lambda b, i: (b, 0, i, 0))
    vals = lambda r: ((batch, g, r, seq), BF16, (1, g, r, tm), lambda b, i: (b, 0, 0, i))
    outs = [rows(nq, BF16), rows(nq, BF16), keys(2 * LANES), keys(LANES), rows(nk, F32),
            rows(nk, F32), vals(HEAD_DIM + ONES_ROWS), vals(HEAD_DIM), rows(LANES, F32)]
    return _proj_call(_nsa_proj_kernel, "nsa_proj", batch, seq, (x2, cosf, sinf),
                      (mod, gln, wn, wr, wg, bg, e, et, gain), outs)


def _diff_proj(x2, mod, gln, wn, wr, gain, cosf, sinf, batch, seq):
    t = x2.shape[0]
    tm = TM_PROJ
    nt = seq // tm
    nq = wn.shape[1] // 2
    hw = 2 * HEAD_DIM
    heads = wr.shape[1] // hw
    e, et = _head_indicator(wn.shape[1])
    outs = [((t, nq), BF16, (tm, nq), lambda b, i: (b * nt + i, 0)),
            ((batch, seq, nq), BF16, (1, tm, nq), lambda b, i: (b, i, 0)),
            ((batch, heads, hw + ONES_ROWS, seq), BF16, (1, heads, hw + ONES_ROWS, tm),
             lambda b, i: (b, 0, 0, i))]
    return _proj_call(_diff_proj_kernel, "diff_proj", batch, seq, (x2, cosf, sinf),
                      (mod, gln, wn, wr, e, et, gain), outs)


def _compress_hidden(x, pe_ref, w1_ref):
    half = x.shape[1]
    za = (x + pe_ref[0:1, :]).astype(BF16)
    zb = (x + pe_ref[1:2, :]).astype(BF16)
    a = _dot(za, w1_ref[0:half, :])
    b = _dot(zb, w1_ref[half:2 * half, :])
    pre = a + pltpu.roll(b, x.shape[0] - 1, 0)
    return (pre * _sigmoid(pre)).astype(BF16)


def _compress_kernel(xk_ref, xv_ref, pek_ref, pev_ref, wk1_ref, wk2_ref, wv1_ref, wv2t_ref,
                     kg_ref, kc_ref, vct_ref):
    k = _dot(_compress_hidden(xk_ref[0, 0], pek_ref, wk1_ref), wk2_ref[...])
    ms = jnp.mean(k * k, axis=-1, keepdims=True)
    kc_ref[0, 0] = (k * lax.rsqrt(ms + EPS) * kg_ref[...]).astype(BF16)
    hv = _compress_hidden(xv_ref[0, 0], pev_ref, wv1_ref)
    vct_ref[0, 0] = _dot_nt(wv2t_ref[...], hv).astype(BF16)


def _nsa_compress(xk, xv, pek, pev, wk1, wk2dup, wv1, wv2t, kgdup):
    b, g, nh, w = xk.shape
    blk = pl.BlockSpec((1, 1, nh, w), lambda i, j: (i, j, 0, 0))
    full = lambda s: pl.BlockSpec(s, lambda i, j: (0,) * len(s))
    return pl.pallas_call(
        _compress_kernel,
        grid=(b, g),
        in_specs=[blk, blk, full(pek.shape), full(pev.shape), full(wk1.shape), full(wk2dup.shape),
                  full(wv1.shape), full(wv2t.shape), full(kgdup.shape)],
        out_specs=[pl.BlockSpec((1, 1, nh, LANES), lambda i, j: (i, j, 0, 0)),
                   pl.BlockSpec((1, 1, HEAD_DIM, nh), lambda i, j: (i, j, 0, 0))],
        out_shape=[jax.ShapeDtypeStruct((b, g, nh, LANES), BF16),
                   jax.ShapeDtypeStruct((b, g, HEAD_DIM, nh), BF16)],
        compiler_params=pltpu.CompilerParams(
            dimension_semantics=("arbitrary", "arbitrary"), vmem_limit_bytes=VMEM_LIMIT),
        name="nsa_compress",
    )(xk, xv, pek, pev, wk1, wk2dup, wv1, wv2t, kgdup)


def _half_masks(rows):
    lane = lax.broadcasted_iota(jnp.int32, (rows, LANES), 1)
    return lane < HEAD_DIM, lane >= HEAD_DIM


def _flash_pipeline(n_full, tk, n_str, score_fn, mask_fn, value_fn, bufs, m_ref, acc_ref):
    m_ref[...] = jnp.full(m_ref.shape, NEG_INF, F32)
    acc_ref[...] = jnp.zeros(acc_ref.shape, F32)

    def produce(buf, k0):
        s_ref, t_ref = buf
        for u in range(n_str):
            st = score_fn(u, k0)
            s_ref[u] = st
            t_ref[u] = jnp.max(st, axis=0, keepdims=True)

    def consume(buf, k0, causal):
        s_ref, t_ref = buf
        for u in range(n_str):
            st = s_ref[u]
            if causal:
                st = jnp.where(mask_fn(k0), st, NEG_INF)
                tmax = jnp.max(st, axis=0, keepdims=True)
            else:
                tmax = t_ref[u]
            m_old = m_ref[u]
            m_new = jnp.maximum(m_old, tmax)
            p = jnp.exp2(st - m_new).astype(BF16)
            acc_ref[u] = jnp.exp2(m_old - m_new) * acc_ref[u] + _dot(value_fn(u, k0), p)
            m_ref[u] = m_new

    produce(bufs[0], 0)

    def pair(jj, carry):
        k0 = pl.multiple_of(jj * (2 * tk), 2 * tk)
        produce(bufs[1], k0 + tk)
        consume(bufs[0], k0, False)
        produce(bufs[0], k0 + 2 * tk)
        consume(bufs[1], k0 + tk, False)
        return carry

    n_pairs = lax.shift_right_logical(n_full, 1)
    lax.fori_loop(0, n_pairs, pair, 0)
    k0 = pl.multiple_of(n_pairs * (2 * tk), 2 * tk)
    odd = (n_full & 1) == 1

    @pl.when(odd)
    def _():
        produce(bufs[1], k0 + tk)
        consume(bufs[0], k0, False)
        consume(bufs[1], k0 + tk, True)

    @pl.when(jnp.logical_not(odd))
    def _():
        consume(bufs[0], k0, True)


def _nsa_attn_kernel(qn_ref, qr_ref, gate_ref, kc_ref, vct_ref, ks_ref, vst_ref, kw_ref, vwt_ref,
                     ovt_ref, o_ref, gt_ref, val_ref, s0_ref, t0_ref, s1_ref, t1_ref, m_ref, acc_ref,
                     *, tq, tk, gs, n_sel):
    g0 = pl.program_id(1) * gs
    i = pl.program_id(2)
    q0 = i * tq
    nqs = NSA_HPG * tq
    gw = NSA_HPG * HEAD_DIM
    ncb = kc_ref.shape[2]
    top_n = float(min(SEL_TOPK, n_sel))
    lo, hi = _half_masks(tq)
    tq_row = q0 + lax.broadcasted_iota(jnp.int32, (1, tq), 1)
    t4 = q0 + (lax.broadcasted_iota(jnp.int32, (1, nqs), 1) & (tq - 1))

    def stack_heads(ref, u):
        parts = []
        for j in range(NSA_HPG):
            c0 = u * gw + (j // 2) * LANES
            pair = ref[:, c0:c0 + LANES]
            parts.append(jnp.where(lo if j % 2 == 0 else hi, pair, jnp.zeros_like(pair)))
        return jnp.concatenate(parts, axis=0)

    qr_s = [stack_heads(qr_ref, u) for u in range(gs)]

    cend = lax.broadcasted_iota(jnp.int32, (ncb, nqs), 0) * CMP_STRIDE + (CMP_BLOCK - 1)
    cmask = cend <= t4
    blk = lax.broadcasted_iota(jnp.int32, (n_sel, tq), 0)
    bt = lax.shift_right_arithmetic(tq_row, int(math.log2(SEL_BLOCK)))
    forced = (blk == 0) | (blk == bt) | (blk == bt - 1)
    ovt = ovt_ref[...]
    o_cmp = []
    for u in range(gs):
        st = jnp.where(cmask, _dot_nt(kc_ref[0, u], stack_heads(qn_ref, u)), NEG_INF)
        p = jnp.where(cmask, jnp.exp2(st - jnp.max(st, axis=0, keepdims=True)), 0.0)
        l = jnp.sum(p, axis=0, keepdims=True)
        p = p * (1.0 / jnp.where(l > 0.0, l, 1.0))
        o_cmp.append(_dot(vct_ref[0, u], p.astype(BF16)))
        pc_sum = p[:, 0:tq]
        for j in range(1, NSA_HPG):
            pc_sum = pc_sum + p[:, j * tq:(j + 1) * tq]
        imp = sum(_dot(ovt, part) for part in _split3(pc_sum))
        val_ref[u] = jnp.where(forced, SEL_FORCE, jnp.where(blk <= bt, imp[0:n_sel, :], -1.0))

    per_it = tq // SEL_BLOCK
    vals = [val_ref[u] for u in range(gs)]

    def rank_body(it, ranks):
        ranks = list(ranks)
        for r in range(per_it):
            jp = it * per_it + r
            for u in range(gs):
                row = val_ref[u, pl.ds(jp, 1), :]
                beats = (row > vals[u]) | ((row == vals[u]) & (blk > jp))
                ranks[u] = ranks[u] + jnp.where(beats, 1.0, 0.0)
        return tuple(ranks)

    ranks = lax.fori_loop(0, i + 1, rank_body,
                          tuple(jnp.zeros((n_sel, tq), F32) for _ in range(gs)))

    q_aug = []
    for u in range(gs):
        selb = jnp.where(ranks[u] < top_n, 0.0, NEG_INF)
        selb = jnp.concatenate([selb, jnp.zeros((LANES - n_sel, tq), F32)], axis=0).T
        selb = selb.astype(BF16)
        q_aug.append(jnp.concatenate(
            [qr_s[u], jnp.concatenate([selb] * NSA_HPG, axis=0)], axis=1))

    _flash_pipeline(
        q0 // tk, tk, gs,
        lambda u, k0: _dot_nt(ks_ref[0, u, pl.ds(k0, tk), :], q_aug[u]),
        lambda k0: k0 + lax.broadcasted_iota(jnp.int32, (tk, nqs), 0) <= t4,
        lambda u, k0: vst_ref[0, u, :, pl.ds(k0, tk)],
        ((s0_ref, t0_ref), (s1_ref, t1_ref)), m_ref, acc_ref)

    nw = WINDOW + tq
    start = pl.multiple_of(jnp.maximum(q0 - WINDOW, 0), tq)
    wpos = start + lax.broadcasted_iota(jnp.int32, (nw, nqs), 0)
    wmask = (wpos <= t4) & (wpos > t4 - WINDOW)
    gt_ref[...] = gate_ref[...].T
    for u in range(gs):
        st = jnp.where(wmask, _dot_nt(kw_ref[0, u, pl.ds(start, nw), :], qr_s[u]), NEG_INF)
        p = jnp.exp2(st - jnp.max(st, axis=0, keepdims=True))
        o_win = _dot(vwt_ref[0, u, :, pl.ds(start, nw)], p.astype(BF16)) * (
            1.0 / jnp.sum(p, axis=0, keepdims=True))
        acc = acc_ref[u]
        o_sel = acc[0:HEAD_DIM] * (1.0 / acc[HEAD_DIM:HEAD_DIM + 1])

        def gate_rows(branch):
            base = branch * NSA_HEADS + (g0 + u) * NSA_HPG
            return jnp.concatenate(
                [gt_ref[pl.ds(base + j, 1), :] for j in range(NSA_HPG)], axis=1)

        o = gate_rows(0) * o_cmp[u] + gate_rows(1) * o_sel + gate_rows(2) * o_win
        o = jnp.concatenate([o[:, j * tq:(j + 1) * tq] for j in range(NSA_HPG)], axis=0)
        o_ref[:, u * gw:(u + 1) * gw] = o.T.astype(BF16)


def _nsa_attn(qn, qr, gates, kc_dup, vct, ks_aug, vst, kw_dup, vwt, ovt):
    tq, tk, gs = TQ_NSA, TK_NSA, GS_NSA
    b, g, s, _ = ks_aug.shape
    t = qn.shape[0]
    gw = NSA_HPG * HEAD_DIM
    nq = s // tq
    n_sel = s // SEL_BLOCK
    nqs = NSA_HPG * tq
    qspec = pl.BlockSpec((tq, gs * gw), lambda bi, gi, i: (bi * nq + i, gi))
    per_bg = lambda a: pl.BlockSpec((1, gs) + a.shape[2:], lambda bi, gi, i: (bi, gi, 0, 0))
    return pl.pallas_call(
        functools.partial(_nsa_attn_kernel, tq=tq, tk=tk, gs=gs, n_sel=n_sel),
        grid=(b, g // gs, nq),
        in_specs=[qspec, qspec,
                  pl.BlockSpec((tq, LANES), lambda bi, gi, i: (bi * nq + i, 0)),
                  per_bg(kc_dup), per_bg(vct), per_bg(ks_aug), per_bg(vst), per_bg(kw_dup),
                  per_bg(vwt), pl.BlockSpec(ovt.shape, lambda bi, gi, i: (0, 0))],
        out_specs=qspec,
        out_shape=jax.ShapeDtypeStruct((t, g * gw), BF16),
        scratch_shapes=[pltpu.VMEM((LANES, tq), F32),
                        pltpu.VMEM((gs, n_sel, tq), F32),
                        pltpu.VMEM((gs, tk, nqs), F32), pltpu.VMEM((gs, 1, nqs), F32),
                        pltpu.VMEM((gs, tk, nqs), F32), pltpu.VMEM((gs, 1, nqs), F32),
                        pltpu.VMEM((gs, 1, nqs), F32),
                        pltpu.VMEM((gs, vst.shape[2], nqs), F32)],
        compiler_params=pltpu.CompilerParams(
            dimension_semantics=("arbitrary", "arbitrary", "arbitrary"),
            vmem_limit_bytes=VMEM_LIMIT),
        name="nsa_attn",
    )(qn, qr, gates, kc_dup, vct, ks_aug, vst, kw_dup, vwt, ovt)


def _diff_attn_kernel(lamv_ref, q_ref, k_ref, vt_ref, sg_ref, o_ref, s0_ref, t0_ref, s1_ref, t1_ref,
                      m_ref, acc_ref, *, tq, hs, lam_init):
    lv = lamv_ref[...]
    lam = (jnp.exp(jnp.sum(lv[0:1] * lv[1:2], axis=-1, keepdims=True))
           - jnp.exp(jnp.sum(lv[2:3] * lv[3:4], axis=-1, keepdims=True)) + lam_init)
    hw = 2 * HEAD_DIM
    lo, hi = _half_masks(tq)
    q_s = []
    for u in range(hs):
        q = q_ref[:, u * hw:(u + 1) * hw]
        zero = jnp.zeros_like(q)
        q_s.append(jnp.concatenate([jnp.where(lo, q, zero), jnp.where(hi, q, zero)], axis=0))

    def diag_mask(k0):
        kpos = lax.broadcasted_iota(jnp.int32, (tq, 2 * tq), 0)
        qpos = lax.broadcasted_iota(jnp.int32, (tq, 2 * tq), 1) & (tq - 1)
        return kpos <= qpos

    _flash_pipeline(
        pl.program_id(2), tq, hs,
        lambda u, k0: _dot_nt(k_ref[0, pl.ds(k0, tq), u * hw:(u + 1) * hw], q_s[u]),
        diag_mask,
        lambda u, k0: vt_ref[0, u, :, pl.ds(k0, tq)],
        ((s0_ref, t0_ref), (s1_ref, t1_ref)), m_ref, acc_ref)

    for u in range(hs):
        acc = acc_ref[u]
        o = acc[0:hw] * (1.0 / acc[hw:hw + 1])
        o = (o[:, 0:tq] - lam * o[:, tq:2 * tq]).T
        ms = jnp.mean(o * o, axis=-1, keepdims=True)
        o_ref[:, u * hw:(u + 1) * hw] = (
            (o * lax.rsqrt(ms + EPS) * sg_ref[...]) * (1.0 - lam_init)).astype(BF16)


def _diff_attn(lamv, q, k, vt, sg, lam_init):
    tq, hs = TQ_DIFF, HS_DIFF
    b, s, w = k.shape
    hw = 2 * HEAD_DIM
    heads = w // hw
    vw = vt.shape[2]
    nq = s // tq
    full = lambda shp: pl.BlockSpec(shp, lambda bi, hi, i: (0,) * len(shp))
    qspec = pl.BlockSpec((tq, hs * hw), lambda bi, hi, i: (bi * nq + i, hi))
    return pl.pallas_call(
        functools.partial(_diff_attn_kernel, tq=tq, hs=hs, lam_init=lam_init),
        grid=(b, heads // hs, nq),
        in_specs=[full(lamv.shape), qspec,
                  pl.BlockSpec((1, s, hs * hw), lambda bi, hi, i: (bi, 0, hi)),
                  pl.BlockSpec((1, hs, vw, s), lambda bi, hi, i: (bi, hi, 0, 0)),
                  full(sg.shape)],
        out_specs=qspec,
        out_shape=jax.ShapeDtypeStruct(q.shape, BF16),
        scratch_shapes=[pltpu.VMEM((hs, tq, 2 * tq), F32), pltpu.VMEM((hs, 1, 2 * tq), F32),
                        pltpu.VMEM((hs, tq, 2 * tq), F32), pltpu.VMEM((hs, 1, 2 * tq), F32),
                        pltpu.VMEM((hs, 1, 2 * tq), F32), pltpu.VMEM((hs, vw, 2 * tq), F32)],
        compiler_params=pltpu.CompilerParams(
            dimension_semantics=("arbitrary", "arbitrary", "arbitrary"),
            vmem_limit_bytes=VMEM_LIMIT),
        name="diff_attn",
    )(lamv, q, k, vt, sg)


def _mlp_kernel(x_ref, a_ref, mod_ref, gln_ref, wo_ref, w1_ref, w2_ref, o_ref,
                x1_ref, h_ref, acc_ref):
    kf = pl.program_id(1)
    m = mod_ref[0]

    @pl.when(kf == 0)
    def _():
        x1 = x_ref[...] + m[2:3] * _dot(a_ref[...], wo_ref[...])
        x1_ref[...] = x1
        h_ref[...] = _mod_rmsnorm(x1, gln_ref[...], m[3:4], m[4:5]).astype(BF16)
        acc_ref[...] = jnp.zeros(acc_ref.shape, F32)

    u = jnp.maximum(_dot(h_ref[...], w1_ref[...]), 0.0)
    acc_ref[...] += _dot((u * u).astype(BF16), w2_ref[...])

    @pl.when(kf == pl.num_programs(1) - 1)
    def _():
        o_ref[...] = x1_ref[...] + m[5:6] * acc_ref[...]


def _outproj_mlp(x2, attn, mod, gln, wo, w1, w2, seq):
    tm, tf = TM_MLP, TF_MLP
    t, d = x2.shape
    dff = w1.shape[1]
    return pl.pallas_call(
        _mlp_kernel,
        grid=(t // tm, dff // tf),
        in_specs=[
            pl.BlockSpec((tm, d), lambda i, k: (i, 0)),
            pl.BlockSpec((tm, d), lambda i, k: (i, 0)),
            pl.BlockSpec((1, 8, d), lambda i, k: ((i * tm) // seq, 0, 0)),
            pl.BlockSpec((1, d), lambda i, k: (0, 0)),
            pl.BlockSpec((d, d), lambda i, k: (0, 0)),
            pl.BlockSpec((d, tf), lambda i, k: (0, k)),
            pl.BlockSpec((tf, d), lambda i, k: (k, 0)),
        ],
        out_specs=pl.BlockSpec((tm, d), lambda i, k: (i, 0)),
        out_shape=jax.ShapeDtypeStruct((t, d), F32),
        scratch_shapes=[pltpu.VMEM((tm, d), F32), pltpu.VMEM((tm, d), BF16),
                        pltpu.VMEM((tm, d), F32)],
        compiler_params=pltpu.CompilerParams(
            dimension_semantics=("arbitrary", "arbitrary"), vmem_limit_bytes=VMEM_LIMIT),
        name="outproj_mlp",
    )(x2, attn, mod, gln, wo, w1, w2)


def _rope_lane_tables(positions):
    inv = ROPE_THETA ** (-jnp.arange(0, ROPE_DIM, 2, dtype=F32) / ROPE_DIM)
    ang = positions.astype(F32)[..., None] * inv
    cos, sin = jnp.cos(ang), jnp.sin(ang)
    rest = HEAD_DIM - ROPE_DIM
    ones = jnp.ones(cos.shape[:-1] + (rest,), F32)
    zeros = jnp.zeros(cos.shape[:-1] + (rest,), F32)
    cosh = jnp.concatenate([cos, cos, ones], axis=-1)
    sinh = jnp.concatenate([-sin, sin, zeros], axis=-1)
    reps = LANES // HEAD_DIM
    t = positions.shape[0] * positions.shape[1]
    return (jnp.tile(cosh, (1, 1, reps)).reshape(t, LANES),
            jnp.tile(sinh, (1, 1, reps)).reshape(t, LANES))


def _tile_gain(g, heads):
    return jnp.tile(g.astype(F32), heads)


def _dup_heads(w):
    h = w.reshape(w.shape[:-1] + (-1, 1, HEAD_DIM))
    return jnp.concatenate([h, h], axis=-2).reshape(w.shape[:-1] + (-1,))


def kernel(x, c, positions, ln_mix_g, ln_mlp_g, w_ada, b_ada, w_mlp_in, w_mlp_out, nsa_w_in, nsa_b_gate, nsa_q_gain, nsa_k_gain, nsa_pe_k, nsa_w_ck1, nsa_w_ck2, nsa_pe_v, nsa_w_cv1, nsa_w_cv2, nsa_w_out, diff_w_in, diff_q_gain, diff_k_gain, diff_lq1, diff_lk1, diff_lq2, diff_lk2, diff_subln_g, diff_w_out):
    b, s, d = x.shape
    depth = w_ada.shape[0]
    t = b * s
    n_sel = s // SEL_BLOCK
    nq, nk = NSA_HEADS * HEAD_DIM, NSA_GROUPS * HEAD_DIM
    assert s % TM_PROJ == 0 and s % TM_MLP == 0 and s % (2 * TK_NSA) == 0
    assert n_sel <= LANES and s >= WINDOW + TQ_NSA and TK_NSA % TQ_NSA == 0

    cosf, sinf = _rope_lane_tables(positions)
    c_pad = jnp.zeros((8, d), F32).at[:b].set(c)
    mod = _ada_mod(c_pad, w_ada, b_ada)
    mod = mod[:, :b].reshape(depth, b, 6, d)
    mod = jnp.pad(mod, ((0, 0), (0, 0), (0, 2), (0, 0)))

    n_half = s // CMP_STRIDE
    cb = jnp.arange(n_half)
    sb = jnp.arange(LANES)
    ovt = ((cb[None, :] * CMP_STRIDE <= sb[:, None] * SEL_BLOCK + SEL_BLOCK - 1)
           & (cb[None, :] * CMP_STRIDE + CMP_BLOCK - 1 >= sb[:, None] * SEL_BLOCK)
           & (sb[:, None] < n_sel)).astype(BF16)

    x2 = x.reshape(t, d)
    for i in range(depth):
        j = i // 2
        gln = ln_mix_g[i].reshape(1, d)
        if i % 2 == 0:
            w = nsa_w_in[j]
            o0 = nq
            wq = w[:, :o0]
            wkc, wvc, wks, wvs, wkw, wvw = [w[:, o0 + r * nk:o0 + (r + 1) * nk] for r in range(6)]
            wgl = w[:, o0 + 6 * nk:]
            wn = jnp.concatenate([wq, _dup_heads(wks), _dup_heads(wkw)], axis=1).astype(BF16)
            wr = jnp.concatenate([wkc, wvc, wvs, wvw], axis=1).astype(BF16)
            wg = wgl.reshape(d, NSA_HEADS, 3).transpose(0, 2, 1).reshape(d, 3 * NSA_HEADS)
            wg = jnp.pad(wg, ((0, 0), (0, LANES - 3 * NSA_HEADS))).astype(BF16)
            bg = nsa_b_gate[j].reshape(NSA_HEADS, 3).T.reshape(1, 3 * NSA_HEADS)
            bg = jnp.pad(bg, ((0, 0), (0, LANES - 3 * NSA_HEADS)))
            gain = jnp.concatenate([_tile_gain(nsa_q_gain[j], NSA_HEADS),
                                    _tile_gain(nsa_k_gain[j, 1], 2 * NSA_GROUPS),
                                    _tile_gain(nsa_k_gain[j, 2], 2 * NSA_GROUPS)]).reshape(1, -1)
            qn, qr, ks_aug, kw_dup, kc, vc, vst, vwt, gates = _nsa_proj(
                x2, mod[i], gln, wn, wr, wg, bg, gain, cosf, sinf, b, s)

            def halfblocks(a):
                a = a.reshape(b, n_half, CMP_STRIDE, NSA_GROUPS, HEAD_DIM)
                return a.transpose(0, 3, 1, 2, 4).reshape(b, NSA_GROUPS, n_half, CMP_STRIDE * HEAD_DIM)

            pek = nsa_pe_k[j].reshape(2, CMP_STRIDE * HEAD_DIM)
            pev = nsa_pe_v[j].reshape(2, CMP_STRIDE * HEAD_DIM)
            kc_dup, vct = _nsa_compress(
                halfblocks(kc), halfblocks(vc), pek, pev,
                nsa_w_ck1[j].astype(BF16), _dup_heads(nsa_w_ck2[j]).astype(BF16),
                nsa_w_cv1[j].astype(BF16), nsa_w_cv2[j].T.astype(BF16),
                _tile_gain(nsa_k_gain[j, 0], 2).reshape(1, 2 * HEAD_DIM))
            attn = _nsa_attn(qn, qr, gates, kc_dup, vct, ks_aug, vst, kw_dup, vwt, ovt)
            wo = nsa_w_out[j]
        else:
            w = diff_w_in[j]
            dw = w.shape[1] // 3
            heads = dw // HEAD_DIM
            wn = w[:, :2 * dw].astype(BF16)
            wr = w[:, 2 * dw:].astype(BF16)
            gain = jnp.concatenate([_tile_gain(diff_q_gain[j], heads),
                                    _tile_gain(diff_k_gain[j], heads)]).reshape(1, -1)
            q, k, vt = _diff_proj(x2, mod[i], gln, wn, wr, gain, cosf, sinf, b, s)
            lamv = jnp.stack([diff_lq1[j], diff_lk1[j], diff_lq2[j], diff_lk2[j]]).astype(F32)
            lamv = jnp.pad(lamv, ((0, 4), (0, LANES - HEAD_DIM)))
            lam_init = 0.8 - 0.6 * math.exp(-0.3 * i)
            attn = _diff_attn(lamv, q, k, vt, diff_subln_g[j].reshape(1, -1), lam_init)
            wo = diff_w_out[j]
        x2 = _outproj_mlp(x2, attn, mod[i], ln_mlp_g[i].reshape(1, d), wo.astype(BF16),
                          w_mlp_in[i].astype(BF16), w_mlp_out[i].astype(BF16), s)
    return x2.reshape(b, s, d)
```

```python
import functools
import math

import jax
import jax.numpy as jnp
from jax import lax
from jax.experimental import pallas as pl
from jax.experimental.pallas import tpu as pltpu

F32 = jnp.float32
BF16 = jnp.bfloat16

LANES = 128
HEAD_DIM = 64
ROPE_DIM = HEAD_DIM // 4
ROPE_THETA = 500000.0
NSA_GROUPS = 4
NSA_HPG = 4
NSA_HEADS = NSA_GROUPS * NSA_HPG
CMP_BLOCK = 32
CMP_STRIDE = 16
CMP_HIDDEN = 2 * HEAD_DIM
SEL_BLOCK = 64
SEL_TOPK = 16
WINDOW = 512
EPS = 1e-6
NEG_INF = -1e30
SEL_FORCE = 1e6
QK_SCALE = math.log2(math.e) / math.sqrt(HEAD_DIM)
ONES_ROWS = 16

VMEM_LIMIT = 48 * 1024 * 1024
TM_PROJ = 512
TM_MLP, TF_MLP = 512, 2048
TQ_NSA, TK_NSA, GS_NSA = 256, 256, 2
TQ_DIFF, HS_DIFF = 256, 4


def _dot(a, b):
    return jnp.dot(a, b, preferred_element_type=F32)


def _dot_nt(a, b):
    return lax.dot_general(a, b, (((1,), (1,)), ((), ())), preferred_element_type=F32)


def _sigmoid(x):
    return 1.0 / (1.0 + jnp.exp(-x))


def _split3(x):
    hi = x.astype(BF16)
    r1 = x - hi.astype(F32)
    mid = r1.astype(BF16)
    lo = (r1 - mid.astype(F32)).astype(BF16)
    return hi, mid, lo


def _mod_rmsnorm(x, g_ln, shift, scale):
    ms = jnp.mean(x * x, axis=-1, keepdims=True)
    return (x * lax.rsqrt(ms + EPS) * g_ln) * (1.0 + scale) + shift


def _head_rmsnorm(a, seg_ref, gain):
    sq = (a * a).astype(BF16)
    seg = seg_ref[...]
    w = seg.shape[0]
    ss = jnp.concatenate(
        [_dot(sq[:, j * w:(j + 1) * w], seg) for j in range(a.shape[1] // w)], axis=1)
    return a * lax.rsqrt(ss * (1.0 / HEAD_DIM) + EPS) * gain


def _rope(a, cosf, sinf):
    lane = lax.broadcasted_iota(jnp.int32, (a.shape[0], LANES), 1)
    first = (lane & (HEAD_DIM - 1)) < (ROPE_DIM // 2)
    half = ROPE_DIM // 2
    outs = []
    for j in range(a.shape[1] // LANES):
        blk = a[:, j * LANES:(j + 1) * LANES]
        partner = jnp.where(first, pltpu.roll(blk, LANES - half, 1), pltpu.roll(blk, half, 1))
        outs.append(blk * cosf + partner * sinf)
    return jnp.concatenate(outs, axis=1)


def _ada_kernel(c_ref, w_ref, b_ref, o_ref):
    c = c_ref[...]
    cond = c * _sigmoid(c)
    w = w_ref[0]
    c_hi = cond.astype(BF16)
    c_lo = (cond - c_hi.astype(F32)).astype(BF16)
    w_hi = w.astype(BF16)
    w_lo = (w - w_hi.astype(F32)).astype(BF16)
    o_ref[0] = _dot(c_hi, w_hi) + _dot(c_lo, w_hi) + _dot(c_hi, w_lo) + b_ref[0]


def _ada_mod(c_pad, w_ada, b_ada):
    depth, d, n = w_ada.shape
    tn = 1536
    return pl.pallas_call(
        _ada_kernel,
        grid=(depth, n // tn),
        in_specs=[
            pl.BlockSpec((8, d), lambda l, j: (0, 0)),
            pl.BlockSpec((1, d, tn), lambda l, j: (l, 0, j)),
            pl.BlockSpec((1, 1, tn), lambda l, j: (l, 0, j)),
        ],
        out_specs=pl.BlockSpec((1, 8, tn), lambda l, j: (l, 0, j)),
        out_shape=jax.ShapeDtypeStruct((depth, 8, n), F32),
        compiler_params=pltpu.CompilerParams(
            dimension_semantics=("arbitrary", "arbitrary"), vmem_limit_bytes=VMEM_LIMIT),
        name="ada_mod",
    )(c_pad, w_ada, b_ada.reshape(depth, 1, n))


def _store_vt(vt_ref, v, dv):
    step = max(dv, LANES)
    for c in range(v.shape[1] // step):
        vt = v[:, c * step:(c + 1) * step].T
        for r in range(step // dv):
            j = c * (step // dv) + r
            vt_ref[0, j, 0:dv, :] = vt[r * dv:(r + 1) * dv, :].astype(BF16)
            if vt_ref.shape[2] > dv:
                vt_ref[0, j, dv:, :] = jnp.ones((vt_ref.shape[2] - dv, v.shape[0]), BF16)


def _nsa_proj_kernel(x_ref, mod_ref, gln_ref, wn_ref, wr_ref, wg_ref, bg_ref, seg_ref,
                     gain_ref, cos_ref, sin_ref,
                     qn_ref, qr_ref, ks_ref, kw_ref, kc_ref, vc_ref, vst_ref, vwt_ref, gate_ref):
    tm = x_ref.shape[0]
    m = mod_ref[0]
    h = _mod_rmsnorm(x_ref[...], gln_ref[...], m[0:1], m[1:2]).astype(BF16)
    a = _dot(h, wn_ref[...])
    an = _head_rmsnorm(a, seg_ref, gain_ref[...])
    ar = _rope(an, cos_ref[...], sin_ref[...])
    nq = qn_ref.shape[1]
    nk = kc_ref.shape[1]
    qn_ref[...] = (an[:, :nq] * QK_SCALE).astype(BF16)
    qr_ref[...] = (ar[:, :nq] * QK_SCALE).astype(BF16)
    pos = pl.program_id(1) * tm + lax.broadcasted_iota(jnp.int32, (tm, LANES), 0)
    lane = lax.broadcasted_iota(jnp.int32, (tm, LANES), 1)
    lo = lane < HEAD_DIM
    blk_onehot = jnp.where(
        lax.shift_right_logical(pos, int(math.log2(SEL_BLOCK))) == lane, 1.0, 0.0).astype(BF16)
    for c in range(nk // LANES):
        for src, dst, tail in ((nq, ks_ref, blk_onehot), (nq + nk, kw_ref, None)):
            x = ar[:, src + c * LANES:src + (c + 1) * LANES]
            xr = pltpu.roll(x, HEAD_DIM, 1)
            dst[0, 2 * c, :, 0:LANES] = jnp.where(lo, x, xr).astype(BF16)
            dst[0, 2 * c + 1, :, 0:LANES] = jnp.where(lo, xr, x).astype(BF16)
            if tail is not None:
                dst[0, 2 * c, :, LANES:2 * LANES] = tail
                dst[0, 2 * c + 1, :, LANES:2 * LANES] = tail
    raw = _dot(h, wr_ref[...])
    kc_ref[...] = raw[:, 0:nk]
    vc_ref[...] = raw[:, nk:2 * nk]
    _store_vt(vst_ref, raw[:, 2 * nk:3 * nk], HEAD_DIM)
    _store_vt(vwt_ref, raw[:, 3 * nk:4 * nk], HEAD_DIM)
    gate_ref[...] = _sigmoid(_dot(h, wg_ref[...]) + bg_ref[...])


def _diff_proj_kernel(x_ref, mod_ref, gln_ref, wn_ref, wr_ref, seg_ref, gain_ref,
                      cos_ref, sin_ref, q_ref, k_ref, vt_ref):
    m = mod_ref[0]
    h = _mod_rmsnorm(x_ref[...], gln_ref[...], m[0:1], m[1:2]).astype(BF16)
    a = _dot(h, wn_ref[...])
    ar = _rope(_head_rmsnorm(a, seg_ref, gain_ref[...]), cos_ref[...], sin_ref[...])
    nq = q_ref.shape[1]
    q_ref[...] = (ar[:, :nq] * QK_SCALE).astype(BF16)
    k_ref[0] = ar[:, nq:].astype(BF16)
    _store_vt(vt_ref, _dot(h, wr_ref[...]), 2 * HEAD_DIM)


def _head_blockdiag():
    head = jnp.arange(2 * LANES) // HEAD_DIM
    return (head[:, None] == head[None, :]).astype(BF16)


def _proj_call(kernel, name, batch, seq, row_inputs, full_inputs, outs):
    tm = TM_PROJ
    nt = seq // tm
    rows = lambda n: pl.BlockSpec((tm, n), lambda b, i: (b * nt + i, 0))
    full = lambda shp: pl.BlockSpec(shp, lambda b, i: (0,) * len(shp))
    x2, cosf, sinf = row_inputs
    mod = full_inputs[0]
    in_specs = ([rows(x2.shape[1]), pl.BlockSpec((1,) + mod.shape[1:], lambda b, i: (b, 0, 0))]
                + [full(a.shape) for a in full_inputs[1:]]
                + [rows(cosf.shape[1]), rows(sinf.shape[1])])
    return pl.pallas_call(
        kernel,
        grid=(batch, nt),
        in_specs=in_specs,
        out_specs=[pl.BlockSpec(blk, imap) for (_, _, blk, imap) in outs],
        out_shape=[jax.ShapeDtypeStruct(shp, dt) for (shp, dt, _, _) in outs],
        compiler_params=pltpu.CompilerParams(
            dimension_semantics=("arbitrary", "arbitrary"), vmem_limit_bytes=VMEM_LIMIT),
        name=name,
    )(x2, *full_inputs, cosf, sinf)


def _nsa_proj(x2, mod, gln, wn, wr, wg, bg, gain, cosf, sinf, batch, seq):
    t = x2.shape[0]
    tm = TM_PROJ
    nt = seq // tm
    nq, nk, g = NSA_HEADS * HEAD_DIM, NSA_GROUPS * HEAD_DIM, NSA_GROUPS
    seg = _head_blockdiag()
    rows = lambda n, dt: ((t, n), dt, (tm, n), lambda b, i: (b * nt + i, 0))
    keys = lambda n: ((batch, g, seq, n), BF16, (1, g, tm, n), lambda b, i: (b, 0, i, 0))
    vals = lambda r: ((batch, g, r, seq), BF16, (1, g, r, tm), lambda b, i: (b, 0, 0, i))
    outs = [rows(nq, BF16), rows(nq, BF16), keys(2 * LANES), keys(LANES), rows(nk, F32),
            rows(nk, F32), vals(HEAD_DIM + ONES_ROWS), vals(HEAD_DIM + ONES_ROWS), rows(LANES, F32)]
    return _proj_call(_nsa_proj_kernel, "nsa_proj", batch, seq, (x2, cosf, sinf),
                      (mod, gln, wn, wr, wg, bg, seg, gain), outs)


def _diff_proj(x2, mod, gln, wn, wr, gain, cosf, sinf, batch, seq):
    t = x2.shape[0]
    tm = TM_PROJ
    nt = seq // tm
    nq = wn.shape[1] // 2
    hw = 2 * HEAD_DIM
    heads = wr.shape[1] // hw
    seg = _head_blockdiag()
    outs = [((t, nq), BF16, (tm, nq), lambda b, i: (b * nt + i, 0)),
            ((batch, seq, nq), BF16, (1, tm, nq), lambda b, i: (b, i, 0)),
            ((batch, heads, hw + ONES_ROWS, seq), BF16, (1, heads, hw + ONES_ROWS, tm),
             lambda b, i: (b, 0, 0, i))]
    return _proj_call(_diff_proj_kernel, "diff_proj", batch, seq, (x2, cosf, sinf),
                      (mod, gln, wn, wr, seg, gain), outs)


def _compress_hidden(x, pe_ref, w1_ref):
    half = x.shape[1]
    za = (x + pe_ref[0:1, :]).astype(BF16)
    zb = (x + pe_ref[1:2, :]).astype(BF16)
    a = _dot(za, w1_ref[0:half, :])
    b = _dot(zb, w1_ref[half:2 * half, :])
    pre = a + pltpu.roll(b, x.shape[0] - 1, 0)
    return (pre * _sigmoid(pre)).astype(BF16)


def _compress_kernel(xk_ref, xv_ref, pek_ref, pev_ref, wk1_ref, wk2_ref, wv1_ref, wv2t_ref,
                     kg_ref, kc_ref, vct_ref):
    k = _dot(_compress_hidden(xk_ref[0, 0], pek_ref, wk1_ref), wk2_ref[...])
    ms = jnp.mean(k * k, axis=-1, keepdims=True)
    kc_ref[0, 0] = (k * lax.rsqrt(ms + EPS) * kg_ref[...]).astype(BF16)
    hv = _compress_hidden(xv_ref[0, 0], pev_ref, wv1_ref)
    vct_ref[0, 0] = _dot_nt(wv2t_ref[...], hv).astype(BF16)


def _nsa_compress(xk, xv, pek, pev, wk1, wk2dup, wv1, wv2t, kgdup):
    b, g, nh, w = xk.shape
    blk = pl.BlockSpec((1, 1, nh, w), lambda i, j: (i, j, 0, 0))
    full = lambda s: pl.BlockSpec(s, lambda i, j: (0,) * len(s))
    return pl.pallas_call(
        _compress_kernel,
        grid=(b, g),
        in_specs=[blk, blk, full(pek.shape), full(pev.shape), full(wk1.shape), full(wk2dup.shape),
                  full(wv1.shape), full(wv2t.shape), full(kgdup.shape)],
        out_specs=[pl.BlockSpec((1, 1, nh, LANES), lambda i, j: (i, j, 0, 0)),
                   pl.BlockSpec((1, 1, HEAD_DIM, nh), lambda i, j: (i, j, 0, 0))],
        out_shape=[jax.ShapeDtypeStruct((b, g, nh, LANES), BF16),
                   jax.ShapeDtypeStruct((b, g, HEAD_DIM, nh), BF16)],
        compiler_params=pltpu.CompilerParams(
            dimension_semantics=("arbitrary", "arbitrary"), vmem_limit_bytes=VMEM_LIMIT),
        name="nsa_compress",
    )(xk, xv, pek, pev, wk1, wk2dup, wv1, wv2t, kgdup)


def _half_masks(rows):
    lane = lax.broadcasted_iota(jnp.int32, (rows, LANES), 1)
    return lane < HEAD_DIM, lane >= HEAD_DIM


def _flash_reset(m_ref, acc_ref):
    m_ref[...] = jnp.full(m_ref.shape, NEG_INF, F32)
    acc_ref[...] = jnp.zeros(acc_ref.shape, F32)


def _flash_produce(buf, u, st, with_max):
    s_ref, t_ref = buf
    s_ref[u] = st
    if with_max:
        t_ref[u] = jnp.max(st, axis=0, keepdims=True)


def _flash_consume(buf, u, mask, vt, m_ref, acc_ref):
    s_ref, t_ref = buf
    st = s_ref[u]
    if mask is None:
        tmax = t_ref[u]
    else:
        st = jnp.where(mask, st, NEG_INF)
        tmax = jnp.max(st, axis=0, keepdims=True)
    m_old = m_ref[u]
    m_new = jnp.maximum(m_old, tmax)
    p = jnp.exp2(st - m_new).astype(BF16)
    acc_ref[u] = jnp.exp2(m_old - m_new) * acc_ref[u] + _dot(vt, p)
    m_ref[u] = m_new


def _flash_pipeline(n_full, tk, n_str, score_fn, mask_fn, value_fn, bufs, m_ref, acc_ref):
    _flash_reset(m_ref, acc_ref)

    def produce(buf, k0):
        for u in range(n_str):
            _flash_produce(buf, u, score_fn(u, k0), True)

    def consume(buf, k0, causal):
        for u in range(n_str):
            _flash_consume(buf, u, mask_fn(k0) if causal else None, value_fn(u, k0),
                           m_ref, acc_ref)

    produce(bufs[0], 0)

    def pair(jj, carry):
        k0 = pl.multiple_of(jj * (2 * tk), 2 * tk)
        produce(bufs[1], k0 + tk)
        consume(bufs[0], k0, False)
        produce(bufs[0], k0 + 2 * tk)
        consume(bufs[1], k0 + tk, False)
        return carry

    n_pairs = lax.shift_right_logical(n_full, 1)
    lax.fori_loop(0, n_pairs, pair, 0)
    k0 = pl.multiple_of(n_pairs * (2 * tk), 2 * tk)
    odd = (n_full & 1) == 1

    @pl.when(odd)
    def _():
        produce(bufs[1], k0 + tk)
        consume(bufs[0], k0, False)
        consume(bufs[1], k0 + tk, True)

    @pl.when(jnp.logical_not(odd))
    def _():
        consume(bufs[0], k0, True)


def _nsa_attn_kernel(qn_ref, qr_ref, gate_ref, kc_ref, vct_ref, ks_ref, vst_ref, kw_ref, vwt_ref,
                     ovt_ref, o_ref, gt_ref, val_ref, pc_ref, s0_ref, t0_ref, s1_ref, t1_ref, m_ref,
                     acc_ref,
                     *, tq, tk, gs, n_sel):
    g0 = pl.program_id(1) * gs
    i = pl.program_id(2)
    q0 = i * tq
    nqs = NSA_HPG * tq
    gw = NSA_HPG * HEAD_DIM
    ncb = kc_ref.shape[2]
    top_n = float(min(SEL_TOPK, n_sel))
    lo, hi = _half_masks(tq)
    tq_row = q0 + lax.broadcasted_iota(jnp.int32, (1, tq), 1)
    t4 = q0 + (lax.broadcasted_iota(jnp.int32, (1, nqs), 1) & (tq - 1))

    def stack_heads(ref, u):
        parts = []
        for j in range(NSA_HPG):
            c0 = u * gw + (j // 2) * LANES
            pair = ref[:, c0:c0 + LANES]
            parts.append(jnp.where(lo if j % 2 == 0 else hi, pair, jnp.zeros_like(pair)))
        return jnp.concatenate(parts, axis=0)

    qr_s = [stack_heads(qr_ref, u) for u in range(gs)]

    cend = lax.broadcasted_iota(jnp.int32, (ncb, nqs), 0) * CMP_STRIDE + (CMP_BLOCK - 1)
    cmask = cend <= t4
    any_blk = t4 >= CMP_BLOCK - 1
    blk = lax.broadcasted_iota(jnp.int32, (n_sel, tq), 0)
    bt = lax.shift_right_arithmetic(tq_row, int(math.log2(SEL_BLOCK)))
    forced = (blk == 0) | (blk == bt) | (blk == bt - 1)
    ovt = ovt_ref[...]
    o_cmp = []
    for u in range(gs):
        st = jnp.where(cmask, _dot_nt(kc_ref[0, u], stack_heads(qn_ref, u)), NEG_INF)
        pc_ref[...] = jnp.exp2(st - jnp.max(st, axis=0, keepdims=True))
        l = jnp.sum(pc_ref[...], axis=0, keepdims=True)
        inv = jnp.where(any_blk, 1.0 / l, 0.0)
        o_cmp.append(_dot(vct_ref[0, u], pc_ref[...].astype(BF16)) * inv)
        pc_sum = pc_ref[:, 0:tq] * inv[:, 0:tq]
        for j in range(1, NSA_HPG):
            cols = slice(j * tq, (j + 1) * tq)
            pc_sum = pc_sum + pc_ref[:, cols] * inv[:, cols]
        imp = sum(_dot(ovt, part) for part in _split3(pc_sum))
        val_ref[u] = jnp.where(forced, SEL_FORCE, jnp.where(blk <= bt, imp[0:n_sel, :], -1.0))

    per_it = tq // SEL_BLOCK
    vals = [val_ref[u] for u in range(gs)]

    def rank_body(it, ranks):
        ranks = list(ranks)
        for r in range(per_it):
            jp = it * per_it + r
            for u in range(gs):
                row = val_ref[u, pl.ds(jp, 1), :]
                beats = (row > vals[u]) | ((row == vals[u]) & (blk > jp))
                ranks[u] = ranks[u] + jnp.where(beats, 1.0, 0.0)
        return tuple(ranks)

    ranks = lax.fori_loop(0, i + 1, rank_body,
                          tuple(jnp.zeros((n_sel, tq), F32) for _ in range(gs)))

    q_aug = []
    for u in range(gs):
        selb = jnp.where(ranks[u] < top_n, 0.0, NEG_INF)
        selb = jnp.concatenate([selb, jnp.zeros((LANES - n_sel, tq), F32)], axis=0).T
        selb = selb.astype(BF16)
        q_aug.append(jnp.concatenate(
            [qr_s[u], jnp.concatenate([selb] * NSA_HPG, axis=0)], axis=1))

    _flash_pipeline(
        q0 // tk, tk, gs,
        lambda u, k0: _dot_nt(ks_ref[0, u, pl.ds(k0, tk), :], q_aug[u]),
        lambda k0: k0 + lax.broadcasted_iota(jnp.int32, (tk, nqs), 0) <= t4,
        lambda u, k0: vst_ref[0, u, :, pl.ds(k0, tk)],
        ((s0_ref, t0_ref), (s1_ref, t1_ref)), m_ref, acc_ref)

    def normalized(u):
        acc = acc_ref[u]
        return acc[0:HEAD_DIM] * (1.0 / acc[HEAD_DIM:HEAD_DIM + 1])

    o_sel = [normalized(u) for u in range(gs)]

    krow = lax.broadcasted_iota(jnp.int32, (tk, nqs), 0)
    qcol = lax.broadcasted_iota(jnp.int32, (tk, nqs), 1) & (tq - 1)
    wtiles = ((jnp.maximum(q0 - 2 * tk, 0), krow > qcol + jnp.where(i >= 2, 0, tk)),
              (jnp.maximum(q0 - tk, 0), krow >= jnp.where(i >= 1, 0, tk)),
              (q0, krow <= qcol))
    bufs = ((s0_ref, t0_ref), (s1_ref, t1_ref))

    def wproduce(buf, w):
        k0 = pl.multiple_of(wtiles[w][0], tk)
        for u in range(gs):
            _flash_produce(buf, u, _dot_nt(kw_ref[0, u, pl.ds(k0, tk), :], qr_s[u]), False)

    def wconsume(buf, w):
        k0 = pl.multiple_of(wtiles[w][0], tk)
        for u in range(gs):
            _flash_consume(buf, u, wtiles[w][1], vwt_ref[0, u, :, pl.ds(k0, tk)], m_ref, acc_ref)

    _flash_reset(m_ref, acc_ref)
    wproduce(bufs[0], 0)
    wproduce(bufs[1], 1)
    wconsume(bufs[0], 0)
    wproduce(bufs[0], 2)
    wconsume(bufs[1], 1)
    wconsume(bufs[0], 2)

    gt_ref[...] = gate_ref[...].T
    for u in range(gs):
        o_win = normalized(u)

        def gate_rows(branch):
            base = branch * NSA_HEADS + (g0 + u) * NSA_HPG
            return jnp.concatenate(
                [gt_ref[pl.ds(base + j, 1), :] for j in range(NSA_HPG)], axis=1)

        o = gate_rows(0) * o_cmp[u] + gate_rows(1) * o_sel[u] + gate_rows(2) * o_win
        o = jnp.concatenate([o[:, j * tq:(j + 1) * tq] for j in range(NSA_HPG)], axis=0)
        o_ref[:, u * gw:(u + 1) * gw] = o.T.astype(BF16)


def _nsa_attn(qn, qr, gates, kc_dup, vct, ks_aug, vst, kw_dup, vwt, ovt):
    tq, tk, gs = TQ_NSA, TK_NSA, GS_NSA
    b, g, s, _ = ks_aug.shape
    t = qn.shape[0]
    gw = NSA_HPG * HEAD_DIM
    nq = s // tq
    n_sel = s // SEL_BLOCK
    nqs = NSA_HPG * tq
    qspec = pl.BlockSpec((tq, gs * gw), lambda bi, gi, i: (bi * nq + i, gi))
    per_bg = lambda a: pl.BlockSpec((1, gs) + a.shape[2:], lambda bi, gi, i: (bi, gi, 0, 0))
    return pl.pallas_call(
        functools.partial(_nsa_attn_kernel, tq=tq, tk=tk, gs=gs, n_sel=n_sel),
        grid=(b, g // gs, nq),
        in_specs=[qspec, qspec,
                  pl.BlockSpec((tq, LANES), lambda bi, gi, i: (bi * nq + i, 0)),
                  per_bg(kc_dup), per_bg(vct), per_bg(ks_aug), per_bg(vst), per_bg(kw_dup),
                  per_bg(vwt), pl.BlockSpec(ovt.shape, lambda bi, gi, i: (0, 0))],
        out_specs=qspec,
        out_shape=jax.ShapeDtypeStruct((t, g * gw), BF16),
        scratch_shapes=[pltpu.VMEM((LANES, tq), F32),
                        pltpu.VMEM((gs, n_sel, tq), F32),
                        pltpu.VMEM((kc_dup.shape[2], nqs), F32),
                        pltpu.VMEM((gs, tk, nqs), F32), pltpu.VMEM((gs, 1, nqs), F32),
                        pltpu.VMEM((gs, tk, nqs), F32), pltpu.VMEM((gs, 1, nqs), F32),
                        pltpu.VMEM((gs, 1, nqs), F32),
                        pltpu.VMEM((gs, vst.shape[2], nqs), F32)],
        compiler_params=pltpu.CompilerParams(
            dimension_semantics=("arbitrary", "arbitrary", "arbitrary"),
            vmem_limit_bytes=VMEM_LIMIT),
        name="nsa_attn",
    )(qn, qr, gates, kc_dup, vct, ks_aug, vst, kw_dup, vwt, ovt)


def _diff_attn_kernel(lamv_ref, q_ref, k_ref, vt_ref, sg_ref, o_ref, s0_ref, t0_ref, s1_ref, t1_ref,
                      m_ref, acc_ref, *, tq, hs, lam_init):
    lv = lamv_ref[...]
    lam = (jnp.exp(jnp.sum(lv[0:1] * lv[1:2], axis=-1, keepdims=True))
           - jnp.exp(jnp.sum(lv[2:3] * lv[3:4], axis=-1, keepdims=True)) + lam_init)
    hw = 2 * HEAD_DIM
    lo, hi = _half_masks(tq)
    q_s = []
    for u in range(hs):
        q = q_ref[:, u * hw:(u + 1) * hw]
        zero = jnp.zeros_like(q)
        q_s.append(jnp.concatenate([jnp.where(lo, q, zero), jnp.where(hi, q, zero)], axis=0))

    def diag_mask(k0):
        kpos = lax.broadcasted_iota(jnp.int32, (tq, 2 * tq), 0)
        qpos = lax.broadcasted_iota(jnp.int32, (tq, 2 * tq), 1) & (tq - 1)
        return kpos <= qpos

    _flash_pipeline(
        pl.program_id(2), tq, hs,
        lambda u, k0: _dot_nt(k_ref[0, pl.ds(k0, tq), u * hw:(u + 1) * hw], q_s[u]),
        diag_mask,
        lambda u, k0: vt_ref[0, u, :, pl.ds(k0, tq)],
        ((s0_ref, t0_ref), (s1_ref, t1_ref)), m_ref, acc_ref)

    for u in range(hs):
        acc = acc_ref[u]
        o = acc[0:hw] * (1.0 / acc[hw:hw + 1])
        o = (o[:, 0:tq] - lam * o[:, tq:2 * tq]).T
        ms = jnp.mean(o * o, axis=-1, keepdims=True)
        o_ref[:, u * hw:(u + 1) * hw] = (
            (o * lax.rsqrt(ms + EPS) * sg_ref[...]) * (1.0 - lam_init)).astype(BF16)


def _diff_attn(lamv, q, k, vt, sg, lam_init):
    tq, hs = TQ_DIFF, HS_DIFF
    b, s, w = k.shape
    hw = 2 * HEAD_DIM
    heads = w // hw
    vw = vt.shape[2]
    nq = s // tq
    full = lambda shp: pl.BlockSpec(shp, lambda bi, hi, i: (0,) * len(shp))
    qspec = pl.BlockSpec((tq, hs * hw), lambda bi, hi, i: (bi * nq + i, hi))
    return pl.pallas_call(
        functools.partial(_diff_attn_kernel, tq=tq, hs=hs, lam_init=lam_init),
        grid=(b, heads // hs, nq),
        in_specs=[full(lamv.shape), qspec,
                  pl.BlockSpec((1, s, hs * hw), lambda bi, hi, i: (bi, 0, hi)),
                  pl.BlockSpec((1, hs, vw, s), lambda bi, hi, i: (bi, hi, 0, 0)),
                  full(sg.shape)],
        out_specs=qspec,
        out_shape=jax.ShapeDtypeStruct(q.shape, BF16),
        scratch_shapes=[pltpu.VMEM((hs, tq, 2 * tq), F32), pltpu.VMEM((hs, 1, 2 * tq), F32),
                        pltpu.VMEM((hs, tq, 2 * tq), F32), pltpu.VMEM((hs, 1, 2 * tq), F32),
                        pltpu.VMEM((hs, 1, 2 * tq), F32), pltpu.VMEM((hs, vw, 2 * tq), F32)],
        compiler_params=pltpu.CompilerParams(
            dimension_semantics=("arbitrary", "arbitrary", "arbitrary"),
            vmem_limit_bytes=VMEM_LIMIT),
        name="diff_attn",
    )(lamv, q, k, vt, sg)


def _mlp_kernel(x_ref, a_ref, mod_ref, gln_ref, wo_ref, w1_ref, w2_ref, o_ref,
                x1_ref, h_ref, acc_ref):
    kf = pl.program_id(1)
    m = mod_ref[0]

    @pl.when(kf == 0)
    def _():
        x1 = x_ref[...] + m[2:3] * _dot(a_ref[...], wo_ref[...])
        x1_ref[...] = x1
        h_ref[...] = _mod_rmsnorm(x1, gln_ref[...], m[3:4], m[4:5]).astype(BF16)
        acc_ref[...] = jnp.zeros(acc_ref.shape, F32)

    u = jnp.maximum(_dot(h_ref[...], w1_ref[...]), 0.0)
    acc_ref[...] += _dot((u * u).astype(BF16), w2_ref[...])

    @pl.when(kf == pl.num_programs(1) - 1)
    def _():
        o_ref[...] = x1_ref[...] + m[5:6] * acc_ref[...]


def _outproj_mlp(x2, attn, mod, gln, wo, w1, w2, seq):
    tm, tf = TM_MLP, TF_MLP
    t, d = x2.shape
    dff = w1.shape[1]
    return pl.pallas_call(
        _mlp_kernel,
        grid=(t // tm, dff // tf),
        in_specs=[
            pl.BlockSpec((tm, d), lambda i, k: (i, 0)),
            pl.BlockSpec((tm, d), lambda i, k: (i, 0)),
            pl.BlockSpec((1, 8, d), lambda i, k: ((i * tm) // seq, 0, 0)),
            pl.BlockSpec((1, d), lambda i, k: (0, 0)),
            pl.BlockSpec((d, d), lambda i, k: (0, 0)),
            pl.BlockSpec((d, tf), lambda i, k: (0, k)),
            pl.BlockSpec((tf, d), lambda i, k: (k, 0)),
        ],
        out_specs=pl.BlockSpec((tm, d), lambda i, k: (i, 0)),
        out_shape=jax.ShapeDtypeStruct((t, d), F32),
        scratch_shapes=[pltpu.VMEM((tm, d), F32), pltpu.VMEM((tm, d), BF16),
                        pltpu.VMEM((tm, d), F32)],
        compiler_params=pltpu.CompilerParams(
            dimension_semantics=("arbitrary", "arbitrary"), vmem_limit_bytes=VMEM_LIMIT),
        name="outproj_mlp",
    )(x2, attn, mod, gln, wo, w1, w2)


def _rope_lane_tables(positions):
    inv = ROPE_THETA ** (-jnp.arange(0, ROPE_DIM, 2, dtype=F32) / ROPE_DIM)
    ang = positions.astype(F32)[..., None] * inv
    cos, sin = jnp.cos(ang), jnp.sin(ang)
    rest = HEAD_DIM - ROPE_DIM
    ones = jnp.ones(cos.shape[:-1] + (rest,), F32)
    zeros = jnp.zeros(cos.shape[:-1] + (rest,), F32)
    cosh = jnp.concatenate([cos, cos, ones], axis=-1)
    sinh = jnp.concatenate([-sin, sin, zeros], axis=-1)
    reps = LANES // HEAD_DIM
    t = positions.shape[0] * positions.shape[1]
    return (jnp.tile(cosh, (1, 1, reps)).reshape(t, LANES),
            jnp.tile(sinh, (1, 1, reps)).reshape(t, LANES))


def _tile_gain(g, heads):
    return jnp.tile(g.astype(F32), heads)


def _dup_heads(w):
    h = w.reshape(w.shape[:-1] + (-1, 1, HEAD_DIM))
    return jnp.concatenate([h, h], axis=-2).reshape(w.shape[:-1] + (-1,))


def kernel(x, c, positions, ln_mix_g, ln_mlp_g, w_ada, b_ada, w_mlp_in, w_mlp_out, nsa_w_in, nsa_b_gate, nsa_q_gain, nsa_k_gain, nsa_pe_k, nsa_w_ck1, nsa_w_ck2, nsa_pe_v, nsa_w_cv1, nsa_w_cv2, nsa_w_out, diff_w_in, diff_q_gain, diff_k_gain, diff_lq1, diff_lk1, diff_lq2, diff_lk2, diff_subln_g, diff_w_out):
    b, s, d = x.shape
    depth = w_ada.shape[0]
    t = b * s
    n_sel = s // SEL_BLOCK
    nq, nk = NSA_HEADS * HEAD_DIM, NSA_GROUPS * HEAD_DIM
    assert s % TM_PROJ == 0 and s % TM_MLP == 0 and s % (2 * TK_NSA) == 0
    assert n_sel <= LANES and TK_NSA == TQ_NSA and WINDOW == 2 * TK_NSA

    cosf, sinf = _rope_lane_tables(positions)
    c_pad = jnp.zeros((8, d), F32).at[:b].set(c)
    mod = _ada_mod(c_pad, w_ada, b_ada)
    mod = mod[:, :b].reshape(depth, b, 6, d)
    mod = jnp.pad(mod, ((0, 0), (0, 0), (0, 2), (0, 0)))

    n_half = s // CMP_STRIDE
    cb = jnp.arange(n_half)
    sb = jnp.arange(LANES)
    ovt = ((cb[None, :] * CMP_STRIDE <= sb[:, None] * SEL_BLOCK + SEL_BLOCK - 1)
           & (cb[None, :] * CMP_STRIDE + CMP_BLOCK - 1 >= sb[:, None] * SEL_BLOCK)
           & (sb[:, None] < n_sel)).astype(BF16)

    x2 = x.reshape(t, d)
    for i in range(depth):
        j = i // 2
        gln = ln_mix_g[i].reshape(1, d)
        if i % 2 == 0:
            w = nsa_w_in[j]
            o0 = nq
            wq = w[:, :o0]
            wkc, wvc, wks, wvs, wkw, wvw = [w[:, o0 + r * nk:o0 + (r + 1) * nk] for r in range(6)]
            wgl = w[:, o0 + 6 * nk:]
            wn = jnp.concatenate([wq, wks, wkw], axis=1).astype(BF16)
            wr = jnp.concatenate([wkc, wvc, wvs, wvw], axis=1).astype(BF16)
            wg = wgl.reshape(d, NSA_HEADS, 3).transpose(0, 2, 1).reshape(d, 3 * NSA_HEADS)
            wg = jnp.pad(wg, ((0, 0), (0, LANES - 3 * NSA_HEADS))).astype(BF16)
            bg = nsa_b_gate[j].reshape(NSA_HEADS, 3).T.reshape(1, 3 * NSA_HEADS)
            bg = jnp.pad(bg, ((0, 0), (0, LANES - 3 * NSA_HEADS)))
            gain = jnp.concatenate([_tile_gain(nsa_q_gain[j], NSA_HEADS),
                                    _tile_gain(nsa_k_gain[j, 1], NSA_GROUPS),
                                    _tile_gain(nsa_k_gain[j, 2], NSA_GROUPS)]).reshape(1, -1)
            qn, qr, ks_aug, kw_dup, kc, vc, vst, vwt, gates = _nsa_proj(
                x2, mod[i], gln, wn, wr, wg, bg, gain, cosf, sinf, b, s)

            def halfblocks(a):
                a = a.reshape(b, n_half, CMP_STRIDE, NSA_GROUPS, HEAD_DIM)
                return a.transpose(0, 3, 1, 2, 4).reshape(b, NSA_GROUPS, n_half, CMP_STRIDE * HEAD_DIM)

            pek = nsa_pe_k[j].reshape(2, CMP_STRIDE * HEAD_DIM)
            pev = nsa_pe_v[j].reshape(2, CMP_STRIDE * HEAD_DIM)
            kc_dup, vct = _nsa_compress(
                halfblocks(kc), halfblocks(vc), pek, pev,
                nsa_w_ck1[j].astype(BF16), _dup_heads(nsa_w_ck2[j]).astype(BF16),
                nsa_w_cv1[j].astype(BF16), nsa_w_cv2[j].T.astype(BF16),
                _tile_gain(nsa_k_gain[j, 0], 2).reshape(1, 2 * HEAD_DIM))
            attn = _nsa_attn(qn, qr, gates, kc_dup, vct, ks_aug, vst, kw_dup, vwt, ovt)
            wo = nsa_w_out[j]
        else:
            w = diff_w_in[j]
            dw = w.shape[1] // 3
            heads = dw // HEAD_DIM
            wn = w[:, :2 * dw].astype(BF16)
            wr = w[:, 2 * dw:].astype(BF16)
            gain = jnp.concatenate([_tile_gain(diff_q_gain[j], heads),
                                    _tile_gain(diff_k_gain[j], heads)]).reshape(1, -1)
            q, k, vt = _diff_proj(x2, mod[i], gln, wn, wr, gain, cosf, sinf, b, s)
            lamv = jnp.stack([diff_lq1[j], diff_lk1[j], diff_lq2[j], diff_lk2[j]]).astype(F32)
            lamv = jnp.pad(lamv, ((0, 4), (0, LANES - HEAD_DIM)))
            lam_init = 0.8 - 0.6 * math.exp(-0.3 * i)
            attn = _diff_attn(lamv, q, k, vt, diff_subln_g[j].reshape(1, -1), lam_init)
            wo = diff_w_out[j]
        x2 = _outproj_mlp(x2, attn, mod[i], ln_mlp_g[i].reshape(1, d), wo.astype(BF16),
                          w_mlp_in[i].astype(BF16), w_mlp_out[i].astype(BF16), s)
    return x2.reshape(b, s, d)
```

```python
import functools
import math

import jax
import jax.numpy as jnp
from jax import lax
from jax.experimental import pallas as pl
from jax.experimental.pallas import tpu as pltpu

F32 = jnp.float32
BF16 = jnp.bfloat16

LANES = 128
HEAD_DIM = 64
ROPE_DIM = HEAD_DIM // 4
ROPE_THETA = 500000.0
NSA_GROUPS = 4
NSA_HPG = 4
NSA_HEADS = NSA_GROUPS * NSA_HPG
CMP_BLOCK = 32
CMP_STRIDE = 16
CMP_HIDDEN = 2 * HEAD_DIM
SEL_BLOCK = 64
SEL_TOPK = 16
WINDOW = 512
EPS = 1e-6
NEG_INF = -1e30
SEL_FORCE = 1e6
QK_SCALE = math.log2(math.e) / math.sqrt(HEAD_DIM)
ONES_ROWS = 16

VMEM_LIMIT = 48 * 1024 * 1024
TM_PROJ = 512
TM_MLP, TF_MLP = 512, 2048
TQ_NSA, TK_NSA, GS_NSA = 256, 256, 2
TQ_DIFF, HS_DIFF = 256, 4


def _dot(a, b):
    return jnp.dot(a, b, preferred_element_type=F32)


def _dot_nt(a, b):
    return lax.dot_general(a, b, (((1,), (1,)), ((), ())), preferred_element_type=F32)


def _sigmoid(x):
    return 1.0 / (1.0 + jnp.exp(-x))


def _split3(x):
    hi = x.astype(BF16)
    r1 = x - hi.astype(F32)
    mid = r1.astype(BF16)
    lo = (r1 - mid.astype(F32)).astype(BF16)
    return hi, mid, lo


def _mod_rmsnorm(x, g_ln, shift, scale):
    ms = jnp.mean(x * x, axis=-1, keepdims=True)
    return (x * lax.rsqrt(ms + EPS) * g_ln) * (1.0 + scale) + shift


def _head_rmsnorm(a, seg_ref, gain):
    sq = (a * a).astype(BF16)
    seg = seg_ref[...]
    w = seg.shape[0]
    ss = jnp.concatenate(
        [_dot(sq[:, j * w:(j + 1) * w], seg) for j in range(a.shape[1] // w)], axis=1)
    return a * lax.rsqrt(ss * (1.0 / HEAD_DIM) + EPS) * gain


def _rope(a, cosf, sinf):
    lane = lax.broadcasted_iota(jnp.int32, (a.shape[0], LANES), 1)
    first = (lane & (HEAD_DIM - 1)) < (ROPE_DIM // 2)
    half = ROPE_DIM // 2
    outs = []
    for j in range(a.shape[1] // LANES):
        blk = a[:, j * LANES:(j + 1) * LANES]
        partner = jnp.where(first, pltpu.roll(blk, LANES - half, 1), pltpu.roll(blk, half, 1))
        outs.append(blk * cosf + partner * sinf)
    return jnp.concatenate(outs, axis=1)


def _ada_kernel(c_ref, w_ref, b_ref, o_ref):
    c = c_ref[...]
    cond = c * _sigmoid(c)
    w = w_ref[0]
    c_hi = cond.astype(BF16)
    c_lo = (cond - c_hi.astype(F32)).astype(BF16)
    w_hi = w.astype(BF16)
    w_lo = (w - w_hi.astype(F32)).astype(BF16)
    o_ref[0] = _dot(c_hi, w_hi) + _dot(c_lo, w_hi) + _dot(c_hi, w_lo) + b_ref[0]


def _ada_mod(c_pad, w_ada, b_ada):
    depth, d, n = w_ada.shape
    tn = 1536
    return pl.pallas_call(
        _ada_kernel,
        grid=(depth, n // tn),
        in_specs=[
            pl.BlockSpec((8, d), lambda l, j: (0, 0)),
            pl.BlockSpec((1, d, tn), lambda l, j: (l, 0, j)),
            pl.BlockSpec((1, 1, tn), lambda l, j: (l, 0, j)),
        ],
        out_specs=pl.BlockSpec((1, 8, tn), lambda l, j: (l, 0, j)),
        out_shape=jax.ShapeDtypeStruct((depth, 8, n), F32),
        compiler_params=pltpu.CompilerParams(
            dimension_semantics=("arbitrary", "arbitrary"), vmem_limit_bytes=VMEM_LIMIT),
        name="ada_mod",
    )(c_pad, w_ada, b_ada.reshape(depth, 1, n))


def _store_vt(vt_ref, v, dv):
    step = max(dv, LANES)
    for c in range(v.shape[1] // step):
        vt = v[:, c * step:(c + 1) * step].T
        for r in range(step // dv):
            j = c * (step // dv) + r
            vt_ref[0, j, 0:dv, :] = vt[r * dv:(r + 1) * dv, :].astype(BF16)
            if vt_ref.shape[2] > dv:
                vt_ref[0, j, dv:, :] = jnp.ones((vt_ref.shape[2] - dv, v.shape[0]), BF16)


def _nsa_proj_kernel(x_ref, mod_ref, gln_ref, wn_ref, wr_ref, wg_ref, bg_ref, seg_ref,
                     gain_ref, cos_ref, sin_ref,
                     qn_ref, qr_ref, ks_ref, kw_ref, kcv_ref, vst_ref, vwt_ref, gate_ref):
    tm = x_ref.shape[0]
    m = mod_ref[0]
    h = _mod_rmsnorm(x_ref[...], gln_ref[...], m[0:1], m[1:2]).astype(BF16)
    a = _dot(h, wn_ref[...])
    an = _head_rmsnorm(a, seg_ref, gain_ref[...])
    ar = _rope(an, cos_ref[...], sin_ref[...])
    nq = qn_ref.shape[1]
    nk = NSA_GROUPS * HEAD_DIM
    qn_ref[...] = (an[:, :nq] * QK_SCALE).astype(BF16)
    qr_ref[...] = (ar[:, :nq] * QK_SCALE).astype(BF16)
    pos = pl.program_id(1) * tm + lax.broadcasted_iota(jnp.int32, (tm, LANES), 0)
    lane = lax.broadcasted_iota(jnp.int32, (tm, LANES), 1)
    lo = lane < HEAD_DIM
    blk_onehot = jnp.where(
        lax.shift_right_logical(pos, int(math.log2(SEL_BLOCK))) == lane, 1.0, 0.0).astype(BF16)
    for c in range(nk // LANES):
        for src, dst, tail in ((nq, ks_ref, blk_onehot), (nq + nk, kw_ref, None)):
            x = ar[:, src + c * LANES:src + (c + 1) * LANES]
            xr = pltpu.roll(x, HEAD_DIM, 1)
            dst[0, 2 * c, :, 0:LANES] = jnp.where(lo, x, xr).astype(BF16)
            dst[0, 2 * c + 1, :, 0:LANES] = jnp.where(lo, xr, x).astype(BF16)
            if tail is not None:
                dst[0, 2 * c, :, LANES:2 * LANES] = tail
                dst[0, 2 * c + 1, :, LANES:2 * LANES] = tail
    raw = _dot(h, wr_ref[...])
    for c in range(kcv_ref.shape[0]):
        kcv_ref[c] = raw[:, c * LANES:(c + 1) * LANES]
    _store_vt(vst_ref, raw[:, 2 * nk:3 * nk], HEAD_DIM)
    _store_vt(vwt_ref, raw[:, 3 * nk:4 * nk], HEAD_DIM)
    gate_ref[...] = _sigmoid(_dot(h, wg_ref[...]) + bg_ref[...])


def _diff_proj_kernel(x_ref, mod_ref, gln_ref, wn_ref, wr_ref, seg_ref, gain_ref,
                      cos_ref, sin_ref, q_ref, k_ref, vt_ref):
    m = mod_ref[0]
    h = _mod_rmsnorm(x_ref[...], gln_ref[...], m[0:1], m[1:2]).astype(BF16)
    a = _dot(h, wn_ref[...])
    ar = _rope(_head_rmsnorm(a, seg_ref, gain_ref[...]), cos_ref[...], sin_ref[...])
    nq = q_ref.shape[1]
    q_ref[...] = (ar[:, :nq] * QK_SCALE).astype(BF16)
    k_ref[0] = ar[:, nq:].astype(BF16)
    _store_vt(vt_ref, _dot(h, wr_ref[...]), 2 * HEAD_DIM)


def _head_blockdiag():
    head = jnp.arange(2 * LANES) // HEAD_DIM
    return (head[:, None] == head[None, :]).astype(BF16)


def _proj_call(kernel, name, batch, seq, row_inputs, full_inputs, outs):
    tm = TM_PROJ
    nt = seq // tm
    rows = lambda n: pl.BlockSpec((tm, n), lambda b, i: (b * nt + i, 0))
    full = lambda shp: pl.BlockSpec(shp, lambda b, i: (0,) * len(shp))
    x2, cosf, sinf = row_inputs
    mod = full_inputs[0]
    in_specs = ([rows(x2.shape[1]), pl.BlockSpec((1,) + mod.shape[1:], lambda b, i: (b, 0, 0))]
                + [full(a.shape) for a in full_inputs[1:]]
                + [rows(cosf.shape[1]), rows(sinf.shape[1])])
    return pl.pallas_call(
        kernel,
        grid=(batch, nt),
        in_specs=in_specs,
        out_specs=[pl.BlockSpec(blk, imap) for (_, _, blk, imap) in outs],
        out_shape=[jax.ShapeDtypeStruct(shp, dt) for (shp, dt, _, _) in outs],
        compiler_params=pltpu.CompilerParams(
            dimension_semantics=("arbitrary", "arbitrary"), vmem_limit_bytes=VMEM_LIMIT),
        name=name,
    )(x2, *full_inputs, cosf, sinf)


def _nsa_proj(x2, mod, gln, wn, wr, wg, bg, gain, cosf, sinf, batch, seq):
    t = x2.shape[0]
    tm = TM_PROJ
    nt = seq // tm
    nq, nk, g = NSA_HEADS * HEAD_DIM, NSA_GROUPS * HEAD_DIM, NSA_GROUPS
    seg = _head_blockdiag()
    rows = lambda n, dt: ((t, n), dt, (tm, n), lambda b, i: (b * nt + i, 0))
    keys = lambda n: ((batch, g, seq, n), BF16, (1, g, tm, n), lambda b, i: (b, 0, i, 0))
    vals = lambda r: ((batch, g, r, seq), BF16, (1, g, r, tm), lambda b, i: (b, 0, 0, i))
    slabs = 2 * nk // LANES
    outs = [rows(nq, BF16), rows(nq, BF16), keys(2 * LANES), keys(LANES),
            ((slabs, t, LANES), F32, (slabs, tm, LANES), lambda b, i: (0, b * nt + i, 0)),
            vals(HEAD_DIM + ONES_ROWS), vals(HEAD_DIM + ONES_ROWS), rows(LANES, F32)]
    return _proj_call(_nsa_proj_kernel, "nsa_proj", batch, seq, (x2, cosf, sinf),
                      (mod, gln, wn, wr, wg, bg, seg, gain), outs)


def _diff_proj(x2, mod, gln, wn, wr, gain, cosf, sinf, batch, seq):
    t = x2.shape[0]
    tm = TM_PROJ
    nt = seq // tm
    nq = wn.shape[1] // 2
    hw = 2 * HEAD_DIM
    heads = wr.shape[1] // hw
    seg = _head_blockdiag()
    outs = [((t, nq), BF16, (tm, nq), lambda b, i: (b * nt + i, 0)),
            ((batch, seq, nq), BF16, (1, tm, nq), lambda b, i: (b, i, 0)),
            ((batch, heads, hw + ONES_ROWS, seq), BF16, (1, heads, hw + ONES_ROWS, tm),
             lambda b, i: (b, 0, 0, i))]
    return _proj_call(_diff_proj_kernel, "diff_proj", batch, seq, (x2, cosf, sinf),
                      (mod, gln, wn, wr, seg, gain), outs)


def _compress_hidden(x_ref, pe_ref, w1_ref):
    nh = x_ref.shape[1] // CMP_STRIDE
    width = w1_ref.shape[2]

    def body(l, carry):
        a, b = carry
        x = x_ref[0, pl.ds(l, nh, stride=CMP_STRIDE), :]
        za = (x + pe_ref[pl.ds(l, 1), :]).astype(BF16)
        zb = (x + pe_ref[pl.ds(l + CMP_STRIDE, 1), :]).astype(BF16)
        return a + _dot(za, w1_ref[l]), b + _dot(zb, w1_ref[l + CMP_STRIDE])

    zero = jnp.zeros((nh, width), F32)
    a, b = lax.fori_loop(0, CMP_STRIDE, body, (zero, zero))
    pre = a + pltpu.roll(b, nh - 1, 0)
    return (pre * _sigmoid(pre)).astype(BF16)


def _compress_kernel(xk_ref, xv_ref, pek_ref, pev_ref, wk1_ref, wk2_ref, wv1_ref, wv2t_ref,
                     kg_ref, kc_ref, vct_ref):
    k = _dot(_compress_hidden(xk_ref, pek_ref, wk1_ref), wk2_ref[...])
    hv = _compress_hidden(xv_ref, pev_ref, wv1_ref)
    for g in range(2):
        kg = k[:, g * LANES:(g + 1) * LANES]
        ms = jnp.mean(kg * kg, axis=-1, keepdims=True)
        kc_ref[0, g] = (kg * lax.rsqrt(ms + EPS) * kg_ref[...]).astype(BF16)
        vct_ref[0, g] = _dot_nt(wv2t_ref[...], hv[:, g * CMP_HIDDEN:(g + 1) * CMP_HIDDEN]).astype(BF16)


def _nsa_compress(kcv, batch, seq, pek, pev, wk1, wk2, wv1, wv2t, kgdup):
    nh = seq // CMP_STRIDE
    pairs = NSA_GROUPS // 2
    full = lambda s: pl.BlockSpec(s, lambda i, j: (0,) * len(s))
    return pl.pallas_call(
        _compress_kernel,
        grid=(batch, pairs),
        in_specs=[pl.BlockSpec((1, seq, LANES), lambda i, j: (j, i, 0)),
                  pl.BlockSpec((1, seq, LANES), lambda i, j: (pairs + j, i, 0)),
                  full(pek.shape), full(pev.shape), full(wk1.shape), full(wk2.shape),
                  full(wv1.shape), full(wv2t.shape), full(kgdup.shape)],
        out_specs=[pl.BlockSpec((1, 2, nh, LANES), lambda i, j: (i, j, 0, 0)),
                   pl.BlockSpec((1, 2, HEAD_DIM, nh), lambda i, j: (i, j, 0, 0))],
        out_shape=[jax.ShapeDtypeStruct((batch, NSA_GROUPS, nh, LANES), BF16),
                   jax.ShapeDtypeStruct((batch, NSA_GROUPS, HEAD_DIM, nh), BF16)],
        compiler_params=pltpu.CompilerParams(
            dimension_semantics=("arbitrary", "arbitrary"), vmem_limit_bytes=VMEM_LIMIT),
        name="nsa_compress",
    )(kcv, kcv, pek, pev, wk1, wk2, wv1, wv2t, kgdup)


def _half_masks(rows):
    lane = lax.broadcasted_iota(jnp.int32, (rows, LANES), 1)
    return lane < HEAD_DIM, lane >= HEAD_DIM


def _flash_reset(m_ref, acc_ref):
    m_ref[...] = jnp.full(m_ref.shape, NEG_INF, F32)
    acc_ref[...] = jnp.zeros(acc_ref.shape, F32)


def _flash_produce(buf, u, st, with_max):
    s_ref, t_ref = buf
    s_ref[u] = st
    if with_max:
        t_ref[u] = jnp.max(st, axis=0, keepdims=True)


def _flash_consume(buf, u, mask, vt, m_ref, acc_ref):
    s_ref, t_ref = buf
    st = s_ref[u]
    if mask is None:
        tmax = t_ref[u]
    else:
        st = jnp.where(mask, st, NEG_INF)
        tmax = jnp.max(st, axis=0, keepdims=True)
    m_old = m_ref[u]
    m_new = jnp.maximum(m_old, tmax)
    p = jnp.exp2(st - m_new).astype(BF16)
    acc_ref[u] = jnp.exp2(m_old - m_new) * acc_ref[u] + _dot(vt, p)
    m_ref[u] = m_new


def _flash_pipeline(n_full, tk, n_str, score_fn, mask_fn, value_fn, bufs, m_ref, acc_ref):
    _flash_reset(m_ref, acc_ref)

    def produce(buf, k0):
        for u in range(n_str):
            _flash_produce(buf, u, score_fn(u, k0), True)

    def consume(buf, k0, causal):
        for u in range(n_str):
            _flash_consume(buf, u, mask_fn(k0) if causal else None, value_fn(u, k0),
                           m_ref, acc_ref)

    produce(bufs[0], 0)

    def pair(jj, carry):
        k0 = pl.multiple_of(jj * (2 * tk), 2 * tk)
        produce(bufs[1], k0 + tk)
        consume(bufs[0], k0, False)
        produce(bufs[0], k0 + 2 * tk)
        consume(bufs[1], k0 + tk, False)
        return carry

    n_pairs = lax.shift_right_logical(n_full, 1)
    lax.fori_loop(0, n_pairs, pair, 0)
    k0 = pl.multiple_of(n_pairs * (2 * tk), 2 * tk)
    odd = (n_full & 1) == 1

    @pl.when(odd)
    def _():
        produce(bufs[1], k0 + tk)
        consume(bufs[0], k0, False)
        consume(bufs[1], k0 + tk, True)

    @pl.when(jnp.logical_not(odd))
    def _():
        consume(bufs[0], k0, True)


def _nsa_attn_kernel(qn_ref, qr_ref, gate_ref, kc_ref, vct_ref, ks_ref, vst_ref, kw_ref, vwt_ref,
                     ovt_ref, o_ref, gt_ref, val_ref, pc_ref, s0_ref, t0_ref, s1_ref, t1_ref, m_ref,
                     acc_ref,
                     *, tq, tk, gs, n_sel):
    g0 = pl.program_id(1) * gs
    i = pl.program_id(2)
    q0 = i * tq
    nqs = NSA_HPG * tq
    gw = NSA_HPG * HEAD_DIM
    ncb = kc_ref.shape[2]
    top_n = float(min(SEL_TOPK, n_sel))
    lo, hi = _half_masks(tq)
    tq_row = q0 + lax.broadcasted_iota(jnp.int32, (1, tq), 1)
    t4 = q0 + (lax.broadcasted_iota(jnp.int32, (1, nqs), 1) & (tq - 1))

    def stack_heads(ref, u):
        parts = []
        for j in range(NSA_HPG):
            c0 = u * gw + (j // 2) * LANES
            pair = ref[:, c0:c0 + LANES]
            parts.append(jnp.where(lo if j % 2 == 0 else hi, pair, jnp.zeros_like(pair)))
        return jnp.concatenate(parts, axis=0)

    qr_s = [stack_heads(qr_ref, u) for u in range(gs)]

    cend = lax.broadcasted_iota(jnp.int32, (ncb, nqs), 0) * CMP_STRIDE + (CMP_BLOCK - 1)
    cmask = cend <= t4
    any_blk = t4 >= CMP_BLOCK - 1
    blk = lax.broadcasted_iota(jnp.int32, (n_sel, tq), 0)
    bt = lax.shift_right_arithmetic(tq_row, int(math.log2(SEL_BLOCK)))
    forced = (blk == 0) | (blk == bt) | (blk == bt - 1)
    ovt = ovt_ref[...]
    o_cmp = []
    for u in range(gs):
        st = jnp.where(cmask, _dot_nt(kc_ref[0, u], stack_heads(qn_ref, u)), NEG_INF)
        pc_ref[...] = jnp.exp2(st - jnp.max(st, axis=0, keepdims=True))
        l = jnp.sum(pc_ref[...], axis=0, keepdims=True)
        inv = jnp.where(any_blk, 1.0 / l, 0.0)
        o_cmp.append(_dot(vct_ref[0, u], pc_ref[...].astype(BF16)) * inv)
        pc_sum = pc_ref[:, 0:tq] * inv[:, 0:tq]
        for j in range(1, NSA_HPG):
            cols = slice(j * tq, (j + 1) * tq)
            pc_sum = pc_sum + pc_ref[:, cols] * inv[:, cols]
        imp = sum(_dot(ovt, part) for part in _split3(pc_sum))
        val_ref[u] = jnp.where(forced, SEL_FORCE, jnp.where(blk <= bt, imp[0:n_sel, :], -1.0))

    per_it = tq // SEL_BLOCK
    vals = [val_ref[u] for u in range(gs)]

    def rank_body(it, ranks):
        ranks = list(ranks)
        for r in range(per_it):
            jp = it * per_it + r
            tie = jnp.where(blk > jp, 1.0, 0.0)
            for u in range(gs):
                row = val_ref[u, pl.ds(jp, 1), :]
                ranks[u] = ranks[u] + jnp.where(row > vals[u], 1.0,
                                                jnp.where(row == vals[u], tie, 0.0))
        return tuple(ranks)

    ranks = lax.fori_loop(0, i + 1, rank_body,
                          tuple(jnp.zeros((n_sel, tq), F32) for _ in range(gs)))

    q_aug = []
    for u in range(gs):
        selb = jnp.where(ranks[u] < top_n, 0.0, NEG_INF)
        selb = jnp.concatenate([selb, jnp.zeros((LANES - n_sel, tq), F32)], axis=0).T
        selb = selb.astype(BF16)
        q_aug.append(jnp.concatenate(
            [qr_s[u], jnp.concatenate([selb] * NSA_HPG, axis=0)], axis=1))

    _flash_pipeline(
        q0 // tk, tk, gs,
        lambda u, k0: _dot_nt(ks_ref[0, u, pl.ds(k0, tk), :], q_aug[u]),
        lambda k0: k0 + lax.broadcasted_iota(jnp.int32, (tk, nqs), 0) <= t4,
        lambda u, k0: vst_ref[0, u, :, pl.ds(k0, tk)],
        ((s0_ref, t0_ref), (s1_ref, t1_ref)), m_ref, acc_ref)

    def normalized(u):
        acc = acc_ref[u]
        return acc[0:HEAD_DIM] * (1.0 / acc[HEAD_DIM:HEAD_DIM + 1])

    o_sel = [normalized(u) for u in range(gs)]

    krow = lax.broadcasted_iota(jnp.int32, (tk, nqs), 0)
    qcol = lax.broadcasted_iota(jnp.int32, (tk, nqs), 1) & (tq - 1)
    wtiles = ((jnp.maximum(q0 - 2 * tk, 0), krow > qcol + jnp.where(i >= 2, 0, tk)),
              (jnp.maximum(q0 - tk, 0), krow >= jnp.where(i >= 1, 0, tk)),
              (q0, krow <= qcol))
    bufs = ((s0_ref, t0_ref), (s1_ref, t1_ref))

    def wproduce(buf, w):
        k0 = pl.multiple_of(wtiles[w][0], tk)
        for u in range(gs):
            _flash_produce(buf, u, _dot_nt(kw_ref[0, u, pl.ds(k0, tk), :], qr_s[u]), False)

    def wconsume(buf, w):
        k0 = pl.multiple_of(wtiles[w][0], tk)
        for u in range(gs):
            _flash_consume(buf, u, wtiles[w][1], vwt_ref[0, u, :, pl.ds(k0, tk)], m_ref, acc_ref)

    _flash_reset(m_ref, acc_ref)
    wproduce(bufs[0], 0)
    wproduce(bufs[1], 1)
    wconsume(bufs[0], 0)
    wproduce(bufs[0], 2)
    wconsume(bufs[1], 1)
    wconsume(bufs[0], 2)

    gt_ref[...] = gate_ref[...].T
    for u in range(gs):
        o_win = normalized(u)

        def gate_rows(branch):
            base = branch * NSA_HEADS + (g0 + u) * NSA_HPG
            return jnp.concatenate(
                [gt_ref[pl.ds(base + j, 1), :] for j in range(NSA_HPG)], axis=1)

        o = gate_rows(0) * o_cmp[u] + gate_rows(1) * o_sel[u] + gate_rows(2) * o_win
        o = jnp.concatenate([o[:, j * tq:(j + 1) * tq] for j in range(NSA_HPG)], axis=0)
        o_ref[:, u * gw:(u + 1) * gw] = o.T.astype(BF16)


def _nsa_attn(qn, qr, gates, kc_dup, vct, ks_aug, vst, kw_dup, vwt, ovt):
    tq, tk, gs = TQ_NSA, TK_NSA, GS_NSA
    b, g, s, _ = ks_aug.shape
    t = qn.shape[0]
    gw = NSA_HPG * HEAD_DIM
    nq = s // tq
    n_sel = s // SEL_BLOCK
    nqs = NSA_HPG * tq
    qspec = pl.BlockSpec((tq, gs * gw), lambda bi, gi, i: (bi * nq + i, gi))
    per_bg = lambda a: pl.BlockSpec((1, gs) + a.shape[2:], lambda bi, gi, i: (bi, gi, 0, 0))
    return pl.pallas_call(
        functools.partial(_nsa_attn_kernel, tq=tq, tk=tk, gs=gs, n_sel=n_sel),
        grid=(b, g // gs, nq),
        in_specs=[qspec, qspec,
                  pl.BlockSpec((tq, LANES), lambda bi, gi, i: (bi * nq + i, 0)),
                  per_bg(kc_dup), per_bg(vct), per_bg(ks_aug), per_bg(vst), per_bg(kw_dup),
                  per_bg(vwt), pl.BlockSpec(ovt.shape, lambda bi, gi, i: (0, 0))],
        out_specs=qspec,
        out_shape=jax.ShapeDtypeStruct((t, g * gw), BF16),
        scratch_shapes=[pltpu.VMEM((LANES, tq), F32),
                        pltpu.VMEM((gs, n_sel, tq), F32),
                        pltpu.VMEM((kc_dup.shape[2], nqs), F32),
                        pltpu.VMEM((gs, tk, nqs), F32), pltpu.VMEM((gs, 1, nqs), F32),
                        pltpu.VMEM((gs, tk, nqs), F32), pltpu.VMEM((gs, 1, nqs), F32),
                        pltpu.VMEM((gs, 1, nqs), F32),
                        pltpu.VMEM((gs, vst.shape[2], nqs), F32)],
        compiler_params=pltpu.CompilerParams(
            dimension_semantics=("arbitrary", "arbitrary", "arbitrary"),
            vmem_limit_bytes=VMEM_LIMIT),
        name="nsa_attn",
    )(qn, qr, gates, kc_dup, vct, ks_aug, vst, kw_dup, vwt, ovt)


def _diff_attn_kernel(lamv_ref, q_ref, k_ref, vt_ref, sg_ref, o_ref, s0_ref, t0_ref, s1_ref, t1_ref,
                      m_ref, acc_ref, *, tq, hs, lam_init):
    lv = lamv_ref[...]
    lam = (jnp.exp(jnp.sum(lv[0:1] * lv[1:2], axis=-1, keepdims=True))
           - jnp.exp(jnp.sum(lv[2:3] * lv[3:4], axis=-1, keepdims=True)) + lam_init)
    hw = 2 * HEAD_DIM
    lo, hi = _half_masks(tq)
    q_s = []
    for u in range(hs):
        q = q_ref[:, u * hw:(u + 1) * hw]
        zero = jnp.zeros_like(q)
        q_s.append(jnp.concatenate([jnp.where(lo, q, zero), jnp.where(hi, q, zero)], axis=0))

    def diag_mask(k0):
        kpos = lax.broadcasted_iota(jnp.int32, (tq, 2 * tq), 0)
        qpos = lax.broadcasted_iota(jnp.int32, (tq, 2 * tq), 1) & (tq - 1)
        return kpos <= qpos

    _flash_pipeline(
        pl.program_id(2), tq, hs,
        lambda u, k0: _dot_nt(k_ref[0, pl.ds(k0, tq), u * hw:(u + 1) * hw], q_s[u]),
        diag_mask,
        lambda u, k0: vt_ref[0, u, :, pl.ds(k0, tq)],
        ((s0_ref, t0_ref), (s1_ref, t1_ref)), m_ref, acc_ref)

    for u in range(hs):
        acc = acc_ref[u]
        o = acc[0:hw] * (1.0 / acc[hw:hw + 1])
        o = (o[:, 0:tq] - lam * o[:, tq:2 * tq]).T
        ms = jnp.mean(o * o, axis=-1, keepdims=True)
        o_ref[:, u * hw:(u + 1) * hw] = (
            (o * lax.rsqrt(ms + EPS) * sg_ref[...]) * (1.0 - lam_init)).astype(BF16)


def _diff_attn(lamv, q, k, vt, sg, lam_init):
    tq, hs = TQ_DIFF, HS_DIFF
    b, s, w = k.shape
    hw = 2 * HEAD_DIM
    heads = w // hw
    vw = vt.shape[2]
    nq = s // tq
    full = lambda shp: pl.BlockSpec(shp, lambda bi, hi, i: (0,) * len(shp))
    qspec = pl.BlockSpec((tq, hs * hw), lambda bi, hi, i: (bi * nq + i, hi))
    return pl.pallas_call(
        functools.partial(_diff_attn_kernel, tq=tq, hs=hs, lam_init=lam_init),
        grid=(b, heads // hs, nq),
        in_specs=[full(lamv.shape), qspec,
                  pl.BlockSpec((1, s, hs * hw), lambda bi, hi, i: (bi, 0, hi)),
                  pl.BlockSpec((1, hs, vw, s), lambda bi, hi, i: (bi, hi, 0, 0)),
                  full(sg.shape)],
        out_specs=qspec,
        out_shape=jax.ShapeDtypeStruct(q.shape, BF16),
        scratch_shapes=[pltpu.VMEM((hs, tq, 2 * tq), F32), pltpu.VMEM((hs, 1, 2 * tq), F32),
                        pltpu.VMEM((hs, tq, 2 * tq), F32), pltpu.VMEM((hs, 1, 2 * tq), F32),
                        pltpu.VMEM((hs, 1, 2 * tq), F32), pltpu.VMEM((hs, vw, 2 * tq), F32)],
        compiler_params=pltpu.CompilerParams(
            dimension_semantics=("arbitrary", "arbitrary", "arbitrary"),
            vmem_limit_bytes=VMEM_LIMIT),
        name="diff_attn",
    )(lamv, q, k, vt, sg)


def _mlp_kernel(x_ref, a_ref, mod_ref, gln_ref, wo_ref, w1_ref, w2_ref, o_ref,
                x1_ref, h_ref, acc_ref):
    kf = pl.program_id(1)
    m = mod_ref[0]

    @pl.when(kf == 0)
    def _():
        x1 = x_ref[...] + m[2:3] * _dot(a_ref[...], wo_ref[...])
        x1_ref[...] = x1
        h_ref[...] = _mod_rmsnorm(x1, gln_ref[...], m[3:4], m[4:5]).astype(BF16)
        acc_ref[...] = jnp.zeros(acc_ref.shape, F32)

    u = jnp.maximum(_dot(h_ref[...], w1_ref[...]), 0.0)
    acc_ref[...] += _dot((u * u).astype(BF16), w2_ref[...])

    @pl.when(kf == pl.num_programs(1) - 1)
    def _():
        o_ref[...] = x1_ref[...] + m[5:6] * acc_ref[...]


def _outproj_mlp(x2, attn, mod, gln, wo, w1, w2, seq):
    tm, tf = TM_MLP, TF_MLP
    t, d = x2.shape
    dff = w1.shape[1]
    return pl.pallas_call(
        _mlp_kernel,
        grid=(t // tm, dff // tf),
        in_specs=[
            pl.BlockSpec((tm, d), lambda i, k: (i, 0)),
            pl.BlockSpec((tm, d), lambda i, k: (i, 0)),
            pl.BlockSpec((1, 8, d), lambda i, k: ((i * tm) // seq, 0, 0)),
            pl.BlockSpec((1, d), lambda i, k: (0, 0)),
            pl.BlockSpec((d, d), lambda i, k: (0, 0)),
            pl.BlockSpec((d, tf), lambda i, k: (0, k)),
            pl.BlockSpec((tf, d), lambda i, k: (k, 0)),
        ],
        out_specs=pl.BlockSpec((tm, d), lambda i, k: (i, 0)),
        out_shape=jax.ShapeDtypeStruct((t, d), F32),
        scratch_shapes=[pltpu.VMEM((tm, d), F32), pltpu.VMEM((tm, d), BF16),
                        pltpu.VMEM((tm, d), F32)],
        compiler_params=pltpu.CompilerParams(
            dimension_semantics=("arbitrary", "arbitrary"), vmem_limit_bytes=VMEM_LIMIT),
        name="outproj_mlp",
    )(x2, attn, mod, gln, wo, w1, w2)


def _rope_lane_tables(positions):
    inv = ROPE_THETA ** (-jnp.arange(0, ROPE_DIM, 2, dtype=F32) / ROPE_DIM)
    ang = positions.astype(F32)[..., None] * inv
    cos, sin = jnp.cos(ang), jnp.sin(ang)
    rest = HEAD_DIM - ROPE_DIM
    ones = jnp.ones(cos.shape[:-1] + (rest,), F32)
    zeros = jnp.zeros(cos.shape[:-1] + (rest,), F32)
    cosh = jnp.concatenate([cos, cos, ones], axis=-1)
    sinh = jnp.concatenate([-sin, sin, zeros], axis=-1)
    reps = LANES // HEAD_DIM
    t = positions.shape[0] * positions.shape[1]
    return (jnp.tile(cosh, (1, 1, reps)).reshape(t, LANES),
            jnp.tile(sinh, (1, 1, reps)).reshape(t, LANES))


def _tile_gain(g, heads):
    return jnp.tile(g.astype(F32), heads)


def _dup_heads(w):
    h = w.reshape(w.shape[:-1] + (-1, 1, HEAD_DIM))
    return jnp.concatenate([h, h], axis=-2).reshape(w.shape[:-1] + (-1,))


def kernel(x, c, positions, ln_mix_g, ln_mlp_g, w_ada, b_ada, w_mlp_in, w_mlp_out, nsa_w_in, nsa_b_gate, nsa_q_gain, nsa_k_gain, nsa_pe_k, nsa_w_ck1, nsa_w_ck2, nsa_pe_v, nsa_w_cv1, nsa_w_cv2, nsa_w_out, diff_w_in, diff_q_gain, diff_k_gain, diff_lq1, diff_lk1, diff_lq2, diff_lk2, diff_subln_g, diff_w_out):
    b, s, d = x.shape
    depth = w_ada.shape[0]
    t = b * s
    n_sel = s // SEL_BLOCK
    nq, nk = NSA_HEADS * HEAD_DIM, NSA_GROUPS * HEAD_DIM
    assert s % TM_PROJ == 0 and s % TM_MLP == 0 and s % (2 * TK_NSA) == 0
    assert n_sel <= LANES and TK_NSA == TQ_NSA and WINDOW == 2 * TK_NSA

    cosf, sinf = _rope_lane_tables(positions)
    c_pad = jnp.zeros((8, d), F32).at[:b].set(c)
    mod = _ada_mod(c_pad, w_ada, b_ada)
    mod = mod[:, :b].reshape(depth, b, 6, d)
    mod = jnp.pad(mod, ((0, 0), (0, 0), (0, 2), (0, 0)))

    n_half = s // CMP_STRIDE
    cb = jnp.arange(n_half)
    sb = jnp.arange(LANES)
    ovt = ((cb[None, :] * CMP_STRIDE <= sb[:, None] * SEL_BLOCK + SEL_BLOCK - 1)
           & (cb[None, :] * CMP_STRIDE + CMP_BLOCK - 1 >= sb[:, None] * SEL_BLOCK)
           & (sb[:, None] < n_sel)).astype(BF16)

    x2 = x.reshape(t, d)
    for i in range(depth):
        j = i // 2
        gln = ln_mix_g[i].reshape(1, d)
        if i % 2 == 0:
            w = nsa_w_in[j]
            o0 = nq
            wq = w[:, :o0]
            wkc, wvc, wks, wvs, wkw, wvw = [w[:, o0 + r * nk:o0 + (r + 1) * nk] for r in range(6)]
            wgl = w[:, o0 + 6 * nk:]
            wn = jnp.concatenate([wq, wks, wkw], axis=1).astype(BF16)
            wr = jnp.concatenate([wkc, wvc, wvs, wvw], axis=1).astype(BF16)
            wg = wgl.reshape(d, NSA_HEADS, 3).transpose(0, 2, 1).reshape(d, 3 * NSA_HEADS)
            wg = jnp.pad(wg, ((0, 0), (0, LANES - 3 * NSA_HEADS))).astype(BF16)
            bg = nsa_b_gate[j].reshape(NSA_HEADS, 3).T.reshape(1, 3 * NSA_HEADS)
            bg = jnp.pad(bg, ((0, 0), (0, LANES - 3 * NSA_HEADS)))
            gain = jnp.concatenate([_tile_gain(nsa_q_gain[j], NSA_HEADS),
                                    _tile_gain(nsa_k_gain[j, 1], NSA_GROUPS),
                                    _tile_gain(nsa_k_gain[j, 2], NSA_GROUPS)]).reshape(1, -1)
            qn, qr, ks_aug, kw_dup, kcv, vst, vwt, gates = _nsa_proj(
                x2, mod[i], gln, wn, wr, wg, bg, gain, cosf, sinf, b, s)

            def pair_blockdiag(w):
                z = jnp.zeros_like(w)
                return jnp.concatenate([jnp.concatenate([w, z], axis=-1),
                                        jnp.concatenate([z, w], axis=-1)], axis=-2)

            def w1_stack(w1):
                return pair_blockdiag(w1.reshape(CMP_BLOCK, HEAD_DIM, CMP_HIDDEN)).astype(BF16)

            kc_dup, vct = _nsa_compress(
                kcv, b, s, jnp.tile(nsa_pe_k[j], (1, 2)), jnp.tile(nsa_pe_v[j], (1, 2)),
                w1_stack(nsa_w_ck1[j]), pair_blockdiag(_dup_heads(nsa_w_ck2[j])).astype(BF16),
                w1_stack(nsa_w_cv1[j]), nsa_w_cv2[j].T.astype(BF16),
                _tile_gain(nsa_k_gain[j, 0], 2).reshape(1, 2 * HEAD_DIM))
            attn = _nsa_attn(qn, qr, gates, kc_dup, vct, ks_aug, vst, kw_dup, vwt, ovt)
            wo = nsa_w_out[j]
        else:
            w = diff_w_in[j]
            dw = w.shape[1] // 3
            heads = dw // HEAD_DIM
            wn = w[:, :2 * dw].astype(BF16)
            wr = w[:, 2 * dw:].astype(BF16)
            gain = jnp.concatenate([_tile_gain(diff_q_gain[j], heads),
                                    _tile_gain(diff_k_gain[j], heads)]).reshape(1, -1)
            q, k, vt = _diff_proj(x2, mod[i], gln, wn, wr, gain, cosf, sinf, b, s)
            lamv = jnp.stack([diff_lq1[j], diff_lk1[j], diff_lq2[j], diff_lk2[j]]).astype(F32)
            lamv = jnp.pad(lamv, ((0, 4), (0, LANES - HEAD_DIM)))
            lam_init = 0.8 - 0.6 * math.exp(-0.3 * i)
            attn = _diff_attn(lamv, q, k, vt, diff_subln_g[j].reshape(1, -1), lam_init)
            wo = diff_w_out[j]
        x2 = _outproj_mlp(x2, attn, mod[i], ln_mlp_g[i].reshape(1, d), wo.astype(BF16),
                          w_mlp_in[i].astype(BF16), w_mlp_out[i].astype(BF16), s)
    return x2.reshape(b, s, d)
```

```python
import functools
import math

import jax
import jax.numpy as jnp
from jax import lax
from jax.experimental import pallas as pl
from jax.experimental.pallas import tpu as pltpu

F32 = jnp.float32
BF16 = jnp.bfloat16

LANES = 128
HEAD_DIM = 64
ROPE_DIM = HEAD_DIM // 4
ROPE_THETA = 500000.0
NSA_GROUPS = 4
NSA_HPG = 4
NSA_HEADS = NSA_GROUPS * NSA_HPG
CMP_BLOCK = 32
CMP_STRIDE = 16
CMP_HIDDEN = 2 * HEAD_DIM
SEL_BLOCK = 64
SEL_TOPK = 16
WINDOW = 512
EPS = 1e-6
NEG_INF = -1e30
SEL_FORCE = 1e6
QK_SCALE = math.log2(math.e) / math.sqrt(HEAD_DIM)
ONES_ROWS = 16

VMEM_LIMIT = 48 * 1024 * 1024
TM_PROJ = 512
TM_MLP, TF_MLP = 512, 2048
MLP_ROW_CHUNKS = 4
TQ_NSA, TK_NSA, GS_NSA = 256, 256, 2
TQ_DIFF, HS_DIFF = 256, 4


def _dot(a, b):
    return jnp.dot(a, b, preferred_element_type=F32)


def _dot_nt(a, b):
    return lax.dot_general(a, b, (((1,), (1,)), ((), ())), preferred_element_type=F32)


def _sigmoid(x):
    return 1.0 / (1.0 + jnp.exp(-x))


def _split3(x):
    hi = x.astype(BF16)
    r1 = x - hi.astype(F32)
    mid = r1.astype(BF16)
    lo = (r1 - mid.astype(F32)).astype(BF16)
    return hi, mid, lo


def _mod_rmsnorm(x, g_ln, shift, scale):
    ms = jnp.mean(x * x, axis=-1, keepdims=True)
    return (x * lax.rsqrt(ms + EPS) * g_ln) * (1.0 + scale) + shift


def _head_rmsnorm(a, seg_ref, gain):
    sq = (a * a).astype(BF16)
    seg = seg_ref[...]
    w = seg.shape[0]
    ss = jnp.concatenate(
        [_dot(sq[:, j * w:(j + 1) * w], seg) for j in range(a.shape[1] // w)], axis=1)
    return a * lax.rsqrt(ss * (1.0 / HEAD_DIM) + EPS) * gain


def _rope(a, cosf, sinf):
    lane = lax.broadcasted_iota(jnp.int32, (a.shape[0], LANES), 1)
    first = (lane & (HEAD_DIM - 1)) < (ROPE_DIM // 2)
    half = ROPE_DIM // 2
    outs = []
    for j in range(a.shape[1] // LANES):
        blk = a[:, j * LANES:(j + 1) * LANES]
        partner = jnp.where(first, pltpu.roll(blk, LANES - half, 1), pltpu.roll(blk, half, 1))
        outs.append(blk * cosf + partner * sinf)
    return jnp.concatenate(outs, axis=1)


def _ada_kernel(c_ref, w_ref, b_ref, o_ref):
    c = c_ref[...]
    cond = c * _sigmoid(c)
    w = w_ref[0]
    c_hi = cond.astype(BF16)
    c_lo = (cond - c_hi.astype(F32)).astype(BF16)
    w_hi = w.astype(BF16)
    w_lo = (w - w_hi.astype(F32)).astype(BF16)
    o_ref[0] = _dot(c_hi, w_hi) + _dot(c_lo, w_hi) + _dot(c_hi, w_lo) + b_ref[0]


def _ada_mod(c_pad, w_ada, b_ada):
    depth, d, n = w_ada.shape
    tn = 1536
    return pl.pallas_call(
        _ada_kernel,
        grid=(depth, n // tn),
        in_specs=[
            pl.BlockSpec((8, d), lambda l, j: (0, 0)),
            pl.BlockSpec((1, d, tn), lambda l, j: (l, 0, j)),
            pl.BlockSpec((1, 1, tn), lambda l, j: (l, 0, j)),
        ],
        out_specs=pl.BlockSpec((1, 8, tn), lambda l, j: (l, 0, j)),
        out_shape=jax.ShapeDtypeStruct((depth, 8, n), F32),
        compiler_params=pltpu.CompilerParams(
            dimension_semantics=("arbitrary", "arbitrary"), vmem_limit_bytes=VMEM_LIMIT),
        name="ada_mod",
    )(c_pad, w_ada, b_ada.reshape(depth, 1, n))


def _store_vt(vt_ref, v, dv):
    step = max(dv, LANES)
    for c in range(v.shape[1] // step):
        vt = v[:, c * step:(c + 1) * step].T
        for r in range(step // dv):
            j = c * (step // dv) + r
            vt_ref[0, j, 0:dv, :] = vt[r * dv:(r + 1) * dv, :].astype(BF16)
            if vt_ref.shape[2] > dv:
                vt_ref[0, j, dv:, :] = jnp.ones((vt_ref.shape[2] - dv, v.shape[0]), BF16)


def _nsa_proj_kernel(x_ref, mod_ref, gln_ref, wn_ref, wr_ref, wg_ref, bg_ref, seg_ref,
                     gain_ref, cos_ref, sin_ref,
                     qn_ref, qr_ref, ks_ref, kw_ref, kcv_ref, vst_ref, vwt_ref, gate_ref):
    tm = x_ref.shape[0]
    m = mod_ref[0]
    h = _mod_rmsnorm(x_ref[...], gln_ref[...], m[0:1], m[1:2]).astype(BF16)
    a = _dot(h, wn_ref[...])
    an = _head_rmsnorm(a, seg_ref, gain_ref[...])
    ar = _rope(an, cos_ref[...], sin_ref[...])
    nq = qn_ref.shape[1]
    nk = NSA_GROUPS * HEAD_DIM
    qn_ref[...] = (an[:, :nq] * QK_SCALE).astype(BF16)
    qr_ref[...] = (ar[:, :nq] * QK_SCALE).astype(BF16)
    pos = pl.program_id(1) * tm + lax.broadcasted_iota(jnp.int32, (tm, LANES), 0)
    lane = lax.broadcasted_iota(jnp.int32, (tm, LANES), 1)
    lo = lane < HEAD_DIM
    blk_onehot = jnp.where(
        lax.shift_right_logical(pos, int(math.log2(SEL_BLOCK))) == lane, 1.0, 0.0).astype(BF16)
    for c in range(nk // LANES):
        for src, dst, tail in ((nq, ks_ref, blk_onehot), (nq + nk, kw_ref, None)):
            x = ar[:, src + c * LANES:src + (c + 1) * LANES]
            xr = pltpu.roll(x, HEAD_DIM, 1)
            dst[0, 2 * c, :, 0:LANES] = jnp.where(lo, x, xr).astype(BF16)
            dst[0, 2 * c + 1, :, 0:LANES] = jnp.where(lo, xr, x).astype(BF16)
            if tail is not None:
                dst[0, 2 * c, :, LANES:2 * LANES] = tail
                dst[0, 2 * c + 1, :, LANES:2 * LANES] = tail
    raw = _dot(h, wr_ref[...])
    for c in range(kcv_ref.shape[0]):
        kcv_ref[c] = raw[:, c * LANES:(c + 1) * LANES]
    _store_vt(vst_ref, raw[:, 2 * nk:3 * nk], HEAD_DIM)
    _store_vt(vwt_ref, raw[:, 3 * nk:4 * nk], HEAD_DIM)
    gate_ref[...] = _sigmoid(_dot(h, wg_ref[...]) + bg_ref[...])


def _diff_proj_kernel(x_ref, mod_ref, gln_ref, wn_ref, wr_ref, seg_ref, gain_ref,
                      cos_ref, sin_ref, q_ref, k_ref, vt_ref):
    m = mod_ref[0]
    h = _mod_rmsnorm(x_ref[...], gln_ref[...], m[0:1], m[1:2]).astype(BF16)
    a = _dot(h, wn_ref[...])
    ar = _rope(_head_rmsnorm(a, seg_ref, gain_ref[...]), cos_ref[...], sin_ref[...])
    nq = q_ref.shape[1]
    q_ref[...] = (ar[:, :nq] * QK_SCALE).astype(BF16)
    k_ref[0] = ar[:, nq:].astype(BF16)
    _store_vt(vt_ref, _dot(h, wr_ref[...]), 2 * HEAD_DIM)


def _head_blockdiag():
    head = jnp.arange(2 * LANES) // HEAD_DIM
    return (head[:, None] == head[None, :]).astype(BF16)


def _proj_call(kernel, name, batch, seq, row_inputs, full_inputs, outs):
    tm = TM_PROJ
    nt = seq // tm
    rows = lambda n: pl.BlockSpec((tm, n), lambda b, i: (b * nt + i, 0))
    full = lambda shp: pl.BlockSpec(shp, lambda b, i: (0,) * len(shp))
    x2, cosf, sinf = row_inputs
    mod = full_inputs[0]
    in_specs = ([rows(x2.shape[1]), pl.BlockSpec((1,) + mod.shape[1:], lambda b, i: (b, 0, 0))]
                + [full(a.shape) for a in full_inputs[1:]]
                + [rows(cosf.shape[1]), rows(sinf.shape[1])])
    return pl.pallas_call(
        kernel,
        grid=(batch, nt),
        in_specs=in_specs,
        out_specs=[pl.BlockSpec(blk, imap) for (_, _, blk, imap) in outs],
        out_shape=[jax.ShapeDtypeStruct(shp, dt) for (shp, dt, _, _) in outs],
        compiler_params=pltpu.CompilerParams(
            dimension_semantics=("arbitrary", "arbitrary"), vmem_limit_bytes=VMEM_LIMIT),
        name=name,
    )(x2, *full_inputs, cosf, sinf)


def _nsa_proj(x2, mod, gln, wn, wr, wg, bg, gain, cosf, sinf, batch, seq):
    t = x2.shape[0]
    tm = TM_PROJ
    nt = seq // tm
    nq, nk, g = NSA_HEADS * HEAD_DIM, NSA_GROUPS * HEAD_DIM, NSA_GROUPS
    seg = _head_blockdiag()
    rows = lambda n, dt: ((t, n), dt, (tm, n), lambda b, i: (b * nt + i, 0))
    keys = lambda n: ((batch, g, seq, n), BF16, (1, g, tm, n), lambda b, i: (b, 0, i, 0))
    vals = lambda r: ((batch, g, r, seq), BF16, (1, g, r, tm), lambda b, i: (b, 0, 0, i))
    slabs = 2 * nk // LANES
    outs = [rows(nq, BF16), rows(nq, BF16), keys(2 * LANES), keys(LANES),
            ((slabs, t, LANES), F32, (slabs, tm, LANES), lambda b, i: (0, b * nt + i, 0)),
            vals(HEAD_DIM + ONES_ROWS), vals(HEAD_DIM + ONES_ROWS), rows(LANES, F32)]
    return _proj_call(_nsa_proj_kernel, "nsa_proj", batch, seq, (x2, cosf, sinf),
                      (mod, gln, wn, wr, wg, bg, seg, gain), outs)


def _diff_proj(x2, mod, gln, wn, wr, gain, cosf, sinf, batch, seq):
    t = x2.shape[0]
    tm = TM_PROJ
    nt = seq // tm
    nq = wn.shape[1] // 2
    hw = 2 * HEAD_DIM
    heads = wr.shape[1] // hw
    seg = _head_blockdiag()
    outs = [((t, nq), BF16, (tm, nq), lambda b, i: (b * nt + i, 0)),
            ((batch, seq, nq), BF16, (1, tm, nq), lambda b, i: (b, i, 0)),
            ((batch, heads, hw + ONES_ROWS, seq), BF16, (1, heads, hw + ONES_ROWS, tm),
             lambda b, i: (b, 0, 0, i))]
    return _proj_call(_diff_proj_kernel, "diff_proj", batch, seq, (x2, cosf, sinf),
                      (mod, gln, wn, wr, seg, gain), outs)


def _compress_hidden(x_ref, pe_ref, w1_ref):
    nh = x_ref.shape[1] // CMP_STRIDE
    width = w1_ref.shape[2]

    def body(l, carry):
        a, b = carry
        x = x_ref[0, pl.ds(l, nh, stride=CMP_STRIDE), :]
        za = (x + pe_ref[pl.ds(l, 1), :]).astype(BF16)
        zb = (x + pe_ref[pl.ds(l + CMP_STRIDE, 1), :]).astype(BF16)
        return a + _dot(za, w1_ref[l]), b + _dot(zb, w1_ref[l + CMP_STRIDE])

    zero = jnp.zeros((nh, width), F32)
    a, b = lax.fori_loop(0, CMP_STRIDE, body, (zero, zero), unroll=4)
    pre = a + pltpu.roll(b, nh - 1, 0)
    return (pre * _sigmoid(pre)).astype(BF16)


def _compress_kernel(xk_ref, xv_ref, pek_ref, pev_ref, wk1_ref, wk2_ref, wv1_ref, wv2t_ref,
                     kg_ref, kc_ref, vct_ref):
    k = _dot(_compress_hidden(xk_ref, pek_ref, wk1_ref), wk2_ref[...])
    hv = _compress_hidden(xv_ref, pev_ref, wv1_ref)
    for g in range(2):
        kg = k[:, g * LANES:(g + 1) * LANES]
        ms = jnp.mean(kg * kg, axis=-1, keepdims=True)
        kc_ref[0, g] = (kg * lax.rsqrt(ms + EPS) * kg_ref[...]).astype(BF16)
        vct_ref[0, g] = _dot_nt(wv2t_ref[...], hv[:, g * CMP_HIDDEN:(g + 1) * CMP_HIDDEN]).astype(BF16)


def _nsa_compress(kcv, batch, seq, pek, pev, wk1, wk2, wv1, wv2t, kgdup):
    nh = seq // CMP_STRIDE
    pairs = NSA_GROUPS // 2
    full = lambda s: pl.BlockSpec(s, lambda i, j: (0,) * len(s))
    return pl.pallas_call(
        _compress_kernel,
        grid=(batch, pairs),
        in_specs=[pl.BlockSpec((1, seq, LANES), lambda i, j: (j, i, 0)),
                  pl.BlockSpec((1, seq, LANES), lambda i, j: (pairs + j, i, 0)),
                  full(pek.shape), full(pev.shape), full(wk1.shape), full(wk2.shape),
                  full(wv1.shape), full(wv2t.shape), full(kgdup.shape)],
        out_specs=[pl.BlockSpec((1, 2, nh, LANES), lambda i, j: (i, j, 0, 0)),
                   pl.BlockSpec((1, 2, HEAD_DIM, nh), lambda i, j: (i, j, 0, 0))],
        out_shape=[jax.ShapeDtypeStruct((batch, NSA_GROUPS, nh, LANES), BF16),
                   jax.ShapeDtypeStruct((batch, NSA_GROUPS, HEAD_DIM, nh), BF16)],
        compiler_params=pltpu.CompilerParams(
            dimension_semantics=("arbitrary", "arbitrary"), vmem_limit_bytes=VMEM_LIMIT),
        name="nsa_compress",
    )(kcv, kcv, pek, pev, wk1, wk2, wv1, wv2t, kgdup)


def _half_masks(rows):
    lane = lax.broadcasted_iota(jnp.int32, (rows, LANES), 1)
    return lane < HEAD_DIM, lane >= HEAD_DIM


def _flash_reset(m_ref, acc_ref):
    m_ref[...] = jnp.full(m_ref.shape, NEG_INF, F32)
    acc_ref[...] = jnp.zeros(acc_ref.shape, F32)


def _flash_produce(buf, u, st, with_max):
    s_ref, t_ref = buf
    s_ref[u] = st
    if with_max:
        t_ref[u] = jnp.max(st, axis=0, keepdims=True)


def _flash_consume(buf, u, mask, vt, m_ref, acc_ref):
    s_ref, t_ref = buf
    st = s_ref[u]
    if mask is None:
        tmax = t_ref[u]
    else:
        st = jnp.where(mask, st, NEG_INF)
        tmax = jnp.max(st, axis=0, keepdims=True)
    m_old = m_ref[u]
    m_new = jnp.maximum(m_old, tmax)
    p = jnp.exp2(st - m_new).astype(BF16)
    acc_ref[u] = jnp.exp2(m_old - m_new) * acc_ref[u] + _dot(vt, p)
    m_ref[u] = m_new


def _flash_pipeline(n_full, tk, n_str, score_fn, mask_fn, value_fn, bufs, m_ref, acc_ref):
    _flash_reset(m_ref, acc_ref)

    def produce(buf, k0):
        for u in range(n_str):
            _flash_produce(buf, u, score_fn(u, k0), True)

    def consume(buf, k0, causal):
        for u in range(n_str):
            _flash_consume(buf, u, mask_fn(k0) if causal else None, value_fn(u, k0),
                           m_ref, acc_ref)

    produce(bufs[0], 0)

    def pair(jj, carry):
        k0 = pl.multiple_of(jj * (2 * tk), 2 * tk)
        produce(bufs[1], k0 + tk)
        consume(bufs[0], k0, False)
        produce(bufs[0], k0 + 2 * tk)
        consume(bufs[1], k0 + tk, False)
        return carry

    n_pairs = lax.shift_right_logical(n_full, 1)
    lax.fori_loop(0, n_pairs, pair, 0)
    k0 = pl.multiple_of(n_pairs * (2 * tk), 2 * tk)
    odd = (n_full & 1) == 1

    @pl.when(odd)
    def _():
        produce(bufs[1], k0 + tk)
        consume(bufs[0], k0, False)
        consume(bufs[1], k0 + tk, True)

    @pl.when(jnp.logical_not(odd))
    def _():
        consume(bufs[0], k0, True)


def _nsa_attn_kernel(qn_ref, qr_ref, gate_ref, kc_ref, vct_ref, ks_ref, vst_ref, kw_ref, vwt_ref,
                     ovt_ref, o_ref, gt_ref, val_ref, pc_ref, ow_ref, s0_ref, t0_ref, s1_ref, t1_ref,
                     m_ref, acc_ref,
                     *, tq, tk, gs, n_sel):
    g0 = pl.program_id(1) * gs
    i = pl.program_id(2)
    q0 = i * tq
    nqs = NSA_HPG * tq
    gw = NSA_HPG * HEAD_DIM
    ncb = kc_ref.shape[2]
    top_n = float(min(SEL_TOPK, n_sel))
    lo, hi = _half_masks(tq)
    tq_row = q0 + lax.broadcasted_iota(jnp.int32, (1, tq), 1)
    t4 = q0 + (lax.broadcasted_iota(jnp.int32, (1, nqs), 1) & (tq - 1))

    def stack_heads(ref, u):
        parts = []
        for j in range(NSA_HPG):
            c0 = u * gw + (j // 2) * LANES
            pair = ref[:, c0:c0 + LANES]
            parts.append(jnp.where(lo if j % 2 == 0 else hi, pair, jnp.zeros_like(pair)))
        return jnp.concatenate(parts, axis=0)

    qr_s = [stack_heads(qr_ref, u) for u in range(gs)]
    bufs = ((s0_ref, t0_ref), (s1_ref, t1_ref))

    def normalized(u):
        acc = acc_ref[u]
        return acc[0:HEAD_DIM] * (1.0 / acc[HEAD_DIM:HEAD_DIM + 1])

    krow = lax.broadcasted_iota(jnp.int32, (tk, nqs), 0)
    qcol = lax.broadcasted_iota(jnp.int32, (tk, nqs), 1) & (tq - 1)
    wtiles = ((jnp.maximum(q0 - 2 * tk, 0), krow > qcol + jnp.where(i >= 2, 0, tk)),
              (jnp.maximum(q0 - tk, 0), krow >= jnp.where(i >= 1, 0, tk)),
              (q0, krow <= qcol))

    def wproduce(buf, w):
        k0 = pl.multiple_of(wtiles[w][0], tk)
        for u in range(gs):
            _flash_produce(buf, u, _dot_nt(kw_ref[0, u, pl.ds(k0, tk), :], qr_s[u]), False)

    def wconsume(buf, w):
        k0 = pl.multiple_of(wtiles[w][0], tk)
        for u in range(gs):
            _flash_consume(buf, u, wtiles[w][1], vwt_ref[0, u, :, pl.ds(k0, tk)], m_ref, acc_ref)

    _flash_reset(m_ref, acc_ref)
    wproduce(bufs[0], 0)
    wproduce(bufs[1], 1)
    wconsume(bufs[0], 0)
    wproduce(bufs[0], 2)
    wconsume(bufs[1], 1)
    wconsume(bufs[0], 2)
    for u in range(gs):
        ow_ref[u] = normalized(u)

    cend = lax.broadcasted_iota(jnp.int32, (ncb, nqs), 0) * CMP_STRIDE + (CMP_BLOCK - 1)
    cmask = cend <= t4
    any_blk = t4 >= CMP_BLOCK - 1
    blk = lax.broadcasted_iota(jnp.int32, (n_sel, tq), 0)
    bt = lax.shift_right_arithmetic(tq_row, int(math.log2(SEL_BLOCK)))
    forced = (blk == 0) | (blk == bt) | (blk == bt - 1)
    ovt = ovt_ref[...]
    o_cmp = []
    for u in range(gs):
        st = jnp.where(cmask, _dot_nt(kc_ref[0, u], stack_heads(qn_ref, u)), NEG_INF)
        pc_ref[...] = jnp.exp2(st - jnp.max(st, axis=0, keepdims=True))
        l = jnp.sum(pc_ref[...], axis=0, keepdims=True)
        inv = jnp.where(any_blk, 1.0 / l, 0.0)
        o_cmp.append(_dot(vct_ref[0, u], pc_ref[...].astype(BF16)) * inv)
        pc_sum = pc_ref[:, 0:tq] * inv[:, 0:tq]
        for j in range(1, NSA_HPG):
            cols = slice(j * tq, (j + 1) * tq)
            pc_sum = pc_sum + pc_ref[:, cols] * inv[:, cols]
        imp = sum(_dot(ovt, part) for part in _split3(pc_sum))
        val_ref[u] = jnp.where(forced, SEL_FORCE, jnp.where(blk <= bt, imp[0:n_sel, :], -1.0))

    per_it = tq // SEL_BLOCK
    vals = [val_ref[u] for u in range(gs)]

    def rank_body(it, ranks):
        ranks = list(ranks)
        for r in range(per_it):
            jp = it * per_it + r
            tie = jnp.where(blk > jp, 1.0, 0.0)
            for u in range(gs):
                row = val_ref[u, pl.ds(jp, 1), :]
                ranks[u] = ranks[u] + jnp.where(row > vals[u], 1.0,
                                                jnp.where(row == vals[u], tie, 0.0))
        return tuple(ranks)

    ranks = lax.fori_loop(0, i + 1, rank_body,
                          tuple(jnp.zeros((n_sel, tq), F32) for _ in range(gs)))

    q_aug = []
    for u in range(gs):
        selb = jnp.where(ranks[u] < top_n, 0.0, NEG_INF)
        selb = jnp.concatenate([selb, jnp.zeros((LANES - n_sel, tq), F32)], axis=0).T
        selb = selb.astype(BF16)
        q_aug.append(jnp.concatenate(
            [qr_s[u], jnp.concatenate([selb] * NSA_HPG, axis=0)], axis=1))

    _flash_pipeline(
        q0 // tk, tk, gs,
        lambda u, k0: _dot_nt(ks_ref[0, u, pl.ds(k0, tk), :], q_aug[u]),
        lambda k0: k0 + lax.broadcasted_iota(jnp.int32, (tk, nqs), 0) <= t4,
        lambda u, k0: vst_ref[0, u, :, pl.ds(k0, tk)],
        ((s0_ref, t0_ref), (s1_ref, t1_ref)), m_ref, acc_ref)

    gt_ref[...] = gate_ref[...].T
    for u in range(gs):
        o_sel = normalized(u)
        o_win = ow_ref[u]

        def gate_rows(branch):
            base = branch * NSA_HEADS + (g0 + u) * NSA_HPG
            return jnp.concatenate(
                [gt_ref[pl.ds(base + j, 1), :] for j in range(NSA_HPG)], axis=1)

        o = gate_rows(0) * o_cmp[u] + gate_rows(1) * o_sel + gate_rows(2) * o_win
        o = jnp.concatenate([o[:, j * tq:(j + 1) * tq] for j in range(NSA_HPG)], axis=0)
        o_ref[:, u * gw:(u + 1) * gw] = o.T.astype(BF16)


def _nsa_attn(qn, qr, gates, kc_dup, vct, ks_aug, vst, kw_dup, vwt, ovt):
    tq, tk, gs = TQ_NSA, TK_NSA, GS_NSA
    b, g, s, _ = ks_aug.shape
    t = qn.shape[0]
    gw = NSA_HPG * HEAD_DIM
    nq = s // tq
    n_sel = s // SEL_BLOCK
    nqs = NSA_HPG * tq
    qspec = pl.BlockSpec((tq, gs * gw), lambda bi, gi, i: (bi * nq + i, gi))
    per_bg = lambda a: pl.BlockSpec((1, gs) + a.shape[2:], lambda bi, gi, i: (bi, gi, 0, 0))
    return pl.pallas_call(
        functools.partial(_nsa_attn_kernel, tq=tq, tk=tk, gs=gs, n_sel=n_sel),
        grid=(b, g // gs, nq),
        in_specs=[qspec, qspec,
                  pl.BlockSpec((tq, LANES), lambda bi, gi, i: (bi * nq + i, 0)),
                  per_bg(kc_dup), per_bg(vct), per_bg(ks_aug), per_bg(vst), per_bg(kw_dup),
                  per_bg(vwt), pl.BlockSpec(ovt.shape, lambda bi, gi, i: (0, 0))],
        out_specs=qspec,
        out_shape=jax.ShapeDtypeStruct((t, g * gw), BF16),
        scratch_shapes=[pltpu.VMEM((LANES, tq), F32),
                        pltpu.VMEM((gs, n_sel, tq), F32),
                        pltpu.VMEM((kc_dup.shape[2], nqs), F32),
                        pltpu.VMEM((gs, HEAD_DIM, nqs), F32),
                        pltpu.VMEM((gs, tk, nqs), F32), pltpu.VMEM((gs, 1, nqs), F32),
                        pltpu.VMEM((gs, tk, nqs), F32), pltpu.VMEM((gs, 1, nqs), F32),
                        pltpu.VMEM((gs, 1, nqs), F32),
                        pltpu.VMEM((gs, vst.shape[2], nqs), F32)],
        compiler_params=pltpu.CompilerParams(
            dimension_semantics=("arbitrary", "arbitrary", "arbitrary"),
            vmem_limit_bytes=VMEM_LIMIT),
        name="nsa_attn",
    )(qn, qr, gates, kc_dup, vct, ks_aug, vst, kw_dup, vwt, ovt)


def _diff_attn_kernel(lamv_ref, q_ref, k_ref, vt_ref, sg_ref, o_ref, s0_ref, t0_ref, s1_ref, t1_ref,
                      m_ref, acc_ref, *, tq, hs, lam_init):
    lv = lamv_ref[...]
    lam = (jnp.exp(jnp.sum(lv[0:1] * lv[1:2], axis=-1, keepdims=True))
           - jnp.exp(jnp.sum(lv[2:3] * lv[3:4], axis=-1, keepdims=True)) + lam_init)
    hw = 2 * HEAD_DIM
    lo, hi = _half_masks(tq)
    q_s = []
    for u in range(hs):
        q = q_ref[:, u * hw:(u + 1) * hw]
        zero = jnp.zeros_like(q)
        q_s.append(jnp.concatenate([jnp.where(lo, q, zero), jnp.where(hi, q, zero)], axis=0))

    def diag_mask(k0):
        kpos = lax.broadcasted_iota(jnp.int32, (tq, 2 * tq), 0)
        qpos = lax.broadcasted_iota(jnp.int32, (tq, 2 * tq), 1) & (tq - 1)
        return kpos <= qpos

    _flash_pipeline(
        pl.program_id(2), tq, hs,
        lambda u, k0: _dot_nt(k_ref[0, pl.ds(k0, tq), u * hw:(u + 1) * hw], q_s[u]),
        diag_mask,
        lambda u, k0: vt_ref[0, u, :, pl.ds(k0, tq)],
        ((s0_ref, t0_ref), (s1_ref, t1_ref)), m_ref, acc_ref)

    for u in range(hs):
        acc = acc_ref[u]
        o = acc[0:hw] * (1.0 / acc[hw:hw + 1])
        o = (o[:, 0:tq] - lam * o[:, tq:2 * tq]).T
        ms = jnp.mean(o * o, axis=-1, keepdims=True)
        o_ref[:, u * hw:(u + 1) * hw] = (
            (o * lax.rsqrt(ms + EPS) * sg_ref[...]) * (1.0 - lam_init)).astype(BF16)


def _diff_attn(lamv, q, k, vt, sg, lam_init):
    tq, hs = TQ_DIFF, HS_DIFF
    b, s, w = k.shape
    hw = 2 * HEAD_DIM
    heads = w // hw
    vw = vt.shape[2]
    nq = s // tq
    full = lambda shp: pl.BlockSpec(shp, lambda bi, hi, i: (0,) * len(shp))
    qspec = pl.BlockSpec((tq, hs * hw), lambda bi, hi, i: (bi * nq + i, hi))
    return pl.pallas_call(
        functools.partial(_diff_attn_kernel, tq=tq, hs=hs, lam_init=lam_init),
        grid=(b, heads // hs, nq),
        in_specs=[full(lamv.shape), qspec,
                  pl.BlockSpec((1, s, hs * hw), lambda bi, hi, i: (bi, 0, hi)),
                  pl.BlockSpec((1, hs, vw, s), lambda bi, hi, i: (bi, hi, 0, 0)),
                  full(sg.shape)],
        out_specs=qspec,
        out_shape=jax.ShapeDtypeStruct(q.shape, BF16),
        scratch_shapes=[pltpu.VMEM((hs, tq, 2 * tq), F32), pltpu.VMEM((hs, 1, 2 * tq), F32),
                        pltpu.VMEM((hs, tq, 2 * tq), F32), pltpu.VMEM((hs, 1, 2 * tq), F32),
                        pltpu.VMEM((hs, 1, 2 * tq), F32), pltpu.VMEM((hs, vw, 2 * tq), F32)],
        compiler_params=pltpu.CompilerParams(
            dimension_semantics=("arbitrary", "arbitrary", "arbitrary"),
            vmem_limit_bytes=VMEM_LIMIT),
        name="diff_attn",
    )(lamv, q, k, vt, sg)


def _mlp_kernel(x_ref, a_ref, mod_ref, gln_ref, wo_ref, w1_ref, w2_ref, o_ref,
                x1_ref, h_ref, acc_ref):
    kf = pl.program_id(1)
    m = mod_ref[0]

    @pl.when(kf == 0)
    def _():
        chunk = x_ref.shape[0] // MLP_ROW_CHUNKS
        rows = [slice(r * chunk, (r + 1) * chunk) for r in range(MLP_ROW_CHUNKS)]
        proj = [_dot(a_ref[r, :], wo_ref[...]) for r in rows]
        for r, pr in zip(rows, proj):
            x1 = x_ref[r, :] + m[2:3] * pr
            x1_ref[r, :] = x1
            h_ref[r, :] = _mod_rmsnorm(x1, gln_ref[...], m[3:4], m[4:5]).astype(BF16)
        acc_ref[...] = jnp.zeros(acc_ref.shape, F32)

    u = jnp.maximum(_dot(h_ref[...], w1_ref[...]), 0.0)
    acc_ref[...] += _dot((u * u).astype(BF16), w2_ref[...])

    @pl.when(kf == pl.num_programs(1) - 1)
    def _():
        o_ref[...] = x1_ref[...] + m[5:6] * acc_ref[...]


def _outproj_mlp(x2, attn, mod, gln, wo, w1, w2, seq):
    tm, tf = TM_MLP, TF_MLP
    t, d = x2.shape
    dff = w1.shape[1]
    return pl.pallas_call(
        _mlp_kernel,
        grid=(t // tm, dff // tf),
        in_specs=[
            pl.BlockSpec((tm, d), lambda i, k: (i, 0)),
            pl.BlockSpec((tm, d), lambda i, k: (i, 0)),
            pl.BlockSpec((1, 8, d), lambda i, k: ((i * tm) // seq, 0, 0)),
            pl.BlockSpec((1, d), lambda i, k: (0, 0)),
            pl.BlockSpec((d, d), lambda i, k: (0, 0)),
            pl.BlockSpec((d, tf), lambda i, k: (0, k)),
            pl.BlockSpec((tf, d), lambda i, k: (k, 0)),
        ],
        out_specs=pl.BlockSpec((tm, d), lambda i, k: (i, 0)),
        out_shape=jax.ShapeDtypeStruct((t, d), F32),
        scratch_shapes=[pltpu.VMEM((tm, d), F32), pltpu.VMEM((tm, d), BF16),
                        pltpu.VMEM((tm, d), F32)],
        compiler_params=pltpu.CompilerParams(
            dimension_semantics=("arbitrary", "arbitrary"), vmem_limit_bytes=VMEM_LIMIT),
        name="outproj_mlp",
    )(x2, attn, mod, gln, wo, w1, w2)


def _rope_lane_tables(positions):
    inv = ROPE_THETA ** (-jnp.arange(0, ROPE_DIM, 2, dtype=F32) / ROPE_DIM)
    ang = positions.astype(F32)[..., None] * inv
    cos, sin = jnp.cos(ang), jnp.sin(ang)
    rest = HEAD_DIM - ROPE_DIM
    ones = jnp.ones(cos.shape[:-1] + (rest,), F32)
    zeros = jnp.zeros(cos.shape[:-1] + (rest,), F32)
    cosh = jnp.concatenate([cos, cos, ones], axis=-1)
    sinh = jnp.concatenate([-sin, sin, zeros], axis=-1)
    reps = LANES // HEAD_DIM
    t = positions.shape[0] * positions.shape[1]
    return (jnp.tile(cosh, (1, 1, reps)).reshape(t, LANES),
            jnp.tile(sinh, (1, 1, reps)).reshape(t, LANES))


def _tile_gain(g, heads):
    return jnp.tile(g.astype(F32), heads)


def _dup_heads(w):
    h = w.reshape(w.shape[:-1] + (-1, 1, HEAD_DIM))
    return jnp.concatenate([h, h], axis=-2).reshape(w.shape[:-1] + (-1,))


def kernel(x, c, positions, ln_mix_g, ln_mlp_g, w_ada, b_ada, w_mlp_in, w_mlp_out, nsa_w_in, nsa_b_gate, nsa_q_gain, nsa_k_gain, nsa_pe_k, nsa_w_ck1, nsa_w_ck2, nsa_pe_v, nsa_w_cv1, nsa_w_cv2, nsa_w_out, diff_w_in, diff_q_gain, diff_k_gain, diff_lq1, diff_lk1, diff_lq2, diff_lk2, diff_subln_g, diff_w_out):
    b, s, d = x.shape
    depth = w_ada.shape[0]
    t = b * s
    n_sel = s // SEL_BLOCK
    nq, nk = NSA_HEADS * HEAD_DIM, NSA_GROUPS * HEAD_DIM
    assert s % TM_PROJ == 0 and s % TM_MLP == 0 and s % (2 * TK_NSA) == 0
    assert n_sel <= LANES and TK_NSA == TQ_NSA and WINDOW == 2 * TK_NSA

    cosf, sinf = _rope_lane_tables(positions)
    c_pad = jnp.zeros((8, d), F32).at[:b].set(c)
    mod = _ada_mod(c_pad, w_ada, b_ada)
    mod = mod[:, :b].reshape(depth, b, 6, d)
    mod = jnp.pad(mod, ((0, 0), (0, 0), (0, 2), (0, 0)))

    n_half = s // CMP_STRIDE
    cb = jnp.arange(n_half)
    sb = jnp.arange(LANES)
    ovt = ((cb[None, :] * CMP_STRIDE <= sb[:, None] * SEL_BLOCK + SEL_BLOCK - 1)
           & (cb[None, :] * CMP_STRIDE + CMP_BLOCK - 1 >= sb[:, None] * SEL_BLOCK)
           & (sb[:, None] < n_sel)).astype(BF16)

    x2 = x.reshape(t, d)
    for i in range(depth):
        j = i // 2
        gln = ln_mix_g[i].reshape(1, d)
        if i % 2 == 0:
            w = nsa_w_in[j]
            o0 = nq
            wq = w[:, :o0]
            wkc, wvc, wks, wvs, wkw, wvw = [w[:, o0 + r * nk:o0 + (r + 1) * nk] for r in range(6)]
            wgl = w[:, o0 + 6 * nk:]
            wn = jnp.concatenate([wq, wks, wkw], axis=1).astype(BF16)
            wr = jnp.concatenate([wkc, wvc, wvs, wvw], axis=1).astype(BF16)
            wg = wgl.reshape(d, NSA_HEADS, 3).transpose(0, 2, 1).reshape(d, 3 * NSA_HEADS)
            wg = jnp.pad(wg, ((0, 0), (0, LANES - 3 * NSA_HEADS))).astype(BF16)
            bg = nsa_b_gate[j].reshape(NSA_HEADS, 3).T.reshape(1, 3 * NSA_HEADS)
            bg = jnp.pad(bg, ((0, 0), (0, LANES - 3 * NSA_HEADS)))
            gain = jnp.concatenate([_tile_gain(nsa_q_gain[j], NSA_HEADS),
                                    _tile_gain(nsa_k_gain[j, 1], NSA_GROUPS),
                                    _tile_gain(nsa_k_gain[j, 2], NSA_GROUPS)]).reshape(1, -1)
            qn, qr, ks_aug, kw_dup, kcv, vst, vwt, gates = _nsa_proj(
                x2, mod[i], gln, wn, wr, wg, bg, gain, cosf, sinf, b, s)

            def pair_blockdiag(w):
                z = jnp.zeros_like(w)
                return jnp.concatenate([jnp.concatenate([w, z], axis=-1),
                                        jnp.concatenate([z, w], axis=-1)], axis=-2)

            def w1_stack(w1):
                return pair_blockdiag(w1.reshape(CMP_BLOCK, HEAD_DIM, CMP_HIDDEN)).astype(BF16)

            kc_dup, vct = _nsa_compress(
                kcv, b, s, jnp.tile(nsa_pe_k[j], (1, 2)), jnp.tile(nsa_pe_v[j], (1, 2)),
                w1_stack(nsa_w_ck1[j]), pair_blockdiag(_dup_heads(nsa_w_ck2[j])).astype(BF16),
                w1_stack(nsa_w_cv1[j]), nsa_w_cv2[j].T.astype(BF16),
                _tile_gain(nsa_k_gain[j, 0], 2).reshape(1, 2 * HEAD_DIM))
            attn = _nsa_attn(qn, qr, gates, kc_dup, vct, ks_aug, vst, kw_dup, vwt, ovt)
            wo = nsa_w_out[j]
        else:
            w = diff_w_in[j]
            dw = w.shape[1] // 3
            heads = dw // HEAD_DIM
            wn = w[:, :2 * dw].astype(BF16)
            wr = w[:, 2 * dw:].astype(BF16)
            gain = jnp.concatenate([_tile_gain(diff_q_gain[j], heads),
                                    _tile_gain(diff_k_gain[j], heads)]).reshape(1, -1)
            q, k, vt = _diff_proj(x2, mod[i], gln, wn, wr, gain, cosf, sinf, b, s)
            lamv = jnp.stack([diff_lq1[j], diff_lk1[j], diff_lq2[j], diff_lk2[j]]).astype(F32)
            lamv = jnp.pad(lamv, ((0, 4), (0, LANES - HEAD_DIM)))
            lam_init = 0.8 - 0.6 * math.exp(-0.3 * i)
            attn = _diff_attn(lamv, q, k, vt, diff_subln_g[j].reshape(1, -1), lam_init)
            wo = diff_w_out[j]
        x2 = _outproj_mlp(x2, attn, mod[i], ln_mlp_g[i].reshape(1, d), wo.astype(BF16),
                          w_mlp_in[i].astype(BF16), w_mlp_out[i].astype(BF16), s)
    return x2.reshape(b, s, d)
```

```python
import functools
import math

import jax
import jax.numpy as jnp
from jax import lax
from jax.experimental import pallas as pl
from jax.experimental.pallas import tpu as pltpu

F32 = jnp.float32
BF16 = jnp.bfloat16

LANES = 128
HEAD_DIM = 64
ROPE_DIM = HEAD_DIM // 4
ROPE_THETA = 500000.0
NSA_GROUPS = 4
NSA_HPG = 4
NSA_HEADS = NSA_GROUPS * NSA_HPG
CMP_BLOCK = 32
CMP_STRIDE = 16
CMP_HIDDEN = 2 * HEAD_DIM
SEL_BLOCK = 64
SEL_TOPK = 16
WINDOW = 512
EPS = 1e-6
NEG_INF = -1e30
SEL_FORCE = 1e6
QK_SCALE = math.log2(math.e) / math.sqrt(HEAD_DIM)
ONES_ROWS = 16

VMEM_LIMIT = 48 * 1024 * 1024
TM_PROJ = 512
TM_MLP, TF_MLP = 512, 2048
MLP_ROW_CHUNKS = 4
TQ_NSA, TK_NSA, GS_NSA = 256, 256, 2
TQ_DIFF, HS_DIFF = 256, 4


def _dot(a, b):
    return jnp.dot(a, b, preferred_element_type=F32)


def _dot_nt(a, b):
    return lax.dot_general(a, b, (((1,), (1,)), ((), ())), preferred_element_type=F32)


def _sigmoid(x):
    return 1.0 / (1.0 + jnp.exp(-x))


def _split3(x):
    hi = x.astype(BF16)
    r1 = x - hi.astype(F32)
    mid = r1.astype(BF16)
    lo = (r1 - mid.astype(F32)).astype(BF16)
    return hi, mid, lo


def _mod_rmsnorm(x, g_ln, shift, scale):
    ms = jnp.mean(x * x, axis=-1, keepdims=True)
    return (x * lax.rsqrt(ms + EPS) * g_ln) * (1.0 + scale) + shift


def _head_rmsnorm(a, seg_ref, gain):
    sq = (a * a).astype(BF16)
    seg = seg_ref[...]
    w = seg.shape[0]
    ss = jnp.concatenate(
        [_dot(sq[:, j * w:(j + 1) * w], seg) for j in range(a.shape[1] // w)], axis=1)
    return a * lax.rsqrt(ss * (1.0 / HEAD_DIM) + EPS) * gain


def _rope(a, cosf, sinf):
    lane = lax.broadcasted_iota(jnp.int32, (a.shape[0], LANES), 1)
    first = (lane & (HEAD_DIM - 1)) < (ROPE_DIM // 2)
    half = ROPE_DIM // 2
    outs = []
    for j in range(a.shape[1] // LANES):
        blk = a[:, j * LANES:(j + 1) * LANES]
        partner = jnp.where(first, pltpu.roll(blk, LANES - half, 1), pltpu.roll(blk, half, 1))
        outs.append(blk * cosf + partner * sinf)
    return jnp.concatenate(outs, axis=1)


def _ada_kernel(c_ref, w_ref, b_ref, o_ref):
    c = c_ref[...]
    cond = c * _sigmoid(c)
    w = w_ref[0]
    c_hi = cond.astype(BF16)
    c_lo = (cond - c_hi.astype(F32)).astype(BF16)
    w_hi = w.astype(BF16)
    w_lo = (w - w_hi.astype(F32)).astype(BF16)
    o_ref[0] = _dot(c_hi, w_hi) + _dot(c_lo, w_hi) + _dot(c_hi, w_lo) + b_ref[0]


def _ada_mod(c_pad, w_ada, b_ada):
    depth, d, n = w_ada.shape
    tn = 1536
    return pl.pallas_call(
        _ada_kernel,
        grid=(depth, n // tn),
        in_specs=[
            pl.BlockSpec((8, d), lambda l, j: (0, 0)),
            pl.BlockSpec((1, d, tn), lambda l, j: (l, 0, j)),
            pl.BlockSpec((1, 1, tn), lambda l, j: (l, 0, j)),
        ],
        out_specs=pl.BlockSpec((1, 8, tn), lambda l, j: (l, 0, j)),
        out_shape=jax.ShapeDtypeStruct((depth, 8, n), F32),
        compiler_params=pltpu.CompilerParams(
            dimension_semantics=("arbitrary", "arbitrary"), vmem_limit_bytes=VMEM_LIMIT),
        name="ada_mod",
    )(c_pad, w_ada, b_ada.reshape(depth, 1, n))


def _store_vt(vt_ref, v, dv):
    step = max(dv, LANES)
    for c in range(v.shape[1] // step):
        vt = v[:, c * step:(c + 1) * step].T
        for r in range(step // dv):
            j = c * (step // dv) + r
            vt_ref[0, j, 0:dv, :] = vt[r * dv:(r + 1) * dv, :].astype(BF16)
            if vt_ref.shape[2] > dv:
                vt_ref[0, j, dv:, :] = jnp.ones((vt_ref.shape[2] - dv, v.shape[0]), BF16)


def _nsa_proj_kernel(x_ref, mod_ref, gln_ref, wn_ref, wr_ref, wg_ref, bg_ref, seg_ref,
                     gain_ref, cos_ref, sin_ref,
                     qn_ref, qr_ref, ks_ref, kw_ref, kcv_ref, vst_ref, vwt_ref, gate_ref):
    tm = x_ref.shape[0]
    m = mod_ref[0]
    h = _mod_rmsnorm(x_ref[...], gln_ref[...], m[0:1], m[1:2]).astype(BF16)
    a = _dot(h, wn_ref[...])
    an = _head_rmsnorm(a, seg_ref, gain_ref[...])
    ar = _rope(an, cos_ref[...], sin_ref[...])
    nq = qn_ref.shape[1]
    nk = NSA_GROUPS * HEAD_DIM
    qn_ref[...] = (an[:, :nq] * QK_SCALE).astype(BF16)
    qr_ref[...] = (ar[:, :nq] * QK_SCALE).astype(BF16)
    pos = pl.program_id(1) * tm + lax.broadcasted_iota(jnp.int32, (tm, LANES), 0)
    lane = lax.broadcasted_iota(jnp.int32, (tm, LANES), 1)
    lo = lane < HEAD_DIM
    blk_onehot = jnp.where(
        lax.shift_right_logical(pos, int(math.log2(SEL_BLOCK))) == lane, 1.0, 0.0).astype(BF16)
    for c in range(nk // LANES):
        for src, dst, tail in ((nq, ks_ref, blk_onehot), (nq + nk, kw_ref, None)):
            x = ar[:, src + c * LANES:src + (c + 1) * LANES]
            xr = pltpu.roll(x, HEAD_DIM, 1)
            dst[0, 2 * c, :, 0:LANES] = jnp.where(lo, x, xr).astype(BF16)
            dst[0, 2 * c + 1, :, 0:LANES] = jnp.where(lo, xr, x).astype(BF16)
            if tail is not None:
                dst[0, 2 * c, :, LANES:2 * LANES] = tail
                dst[0, 2 * c + 1, :, LANES:2 * LANES] = tail
    raw = _dot(h, wr_ref[...])
    for c in range(kcv_ref.shape[0]):
        kcv_ref[c] = raw[:, c * LANES:(c + 1) * LANES]
    _store_vt(vst_ref, raw[:, 2 * nk:3 * nk], HEAD_DIM)
    _store_vt(vwt_ref, raw[:, 3 * nk:4 * nk], HEAD_DIM)
    gate_ref[...] = _sigmoid(_dot(h, wg_ref[...]) + bg_ref[...])


def _diff_proj_kernel(x_ref, mod_ref, gln_ref, wn_ref, wr_ref, seg_ref, gain_ref,
                      cos_ref, sin_ref, q_ref, k_ref, vt_ref):
    m = mod_ref[0]
    h = _mod_rmsnorm(x_ref[...], gln_ref[...], m[0:1], m[1:2]).astype(BF16)
    a = _dot(h, wn_ref[...])
    ar = _rope(_head_rmsnorm(a, seg_ref, gain_ref[...]), cos_ref[...], sin_ref[...])
    nq = q_ref.shape[1]
    q_ref[...] = (ar[:, :nq] * QK_SCALE).astype(BF16)
    k_ref[0] = ar[:, nq:].astype(BF16)
    _store_vt(vt_ref, _dot(h, wr_ref[...]), 2 * HEAD_DIM)


def _head_blockdiag():
    head = jnp.arange(2 * LANES) // HEAD_DIM
    return (head[:, None] == head[None, :]).astype(BF16)


def _proj_call(kernel, name, batch, seq, row_inputs, full_inputs, outs):
    tm = TM_PROJ
    nt = seq // tm
    rows = lambda n: pl.BlockSpec((tm, n), lambda b, i: (b * nt + i, 0))
    full = lambda shp: pl.BlockSpec(shp, lambda b, i: (0,) * len(shp))
    x2, cosf, sinf = row_inputs
    mod = full_inputs[0]
    in_specs = ([rows(x2.shape[1]), pl.BlockSpec((1,) + mod.shape[1:], lambda b, i: (b, 0, 0))]
                + [full(a.shape) for a in full_inputs[1:]]
                + [rows(cosf.shape[1]), rows(sinf.shape[1])])
    return pl.pallas_call(
        kernel,
        grid=(batch, nt),
        in_specs=in_specs,
        out_specs=[pl.BlockSpec(blk, imap) for (_, _, blk, imap) in outs],
        out_shape=[jax.ShapeDtypeStruct(shp, dt) for (shp, dt, _, _) in outs],
        compiler_params=pltpu.CompilerParams(
            dimension_semantics=("arbitrary", "arbitrary"), vmem_limit_bytes=VMEM_LIMIT),
        name=name,
    )(x2, *full_inputs, cosf, sinf)


def _nsa_proj(x2, mod, gln, wn, wr, wg, bg, gain, cosf, sinf, batch, seq):
    t = x2.shape[0]
    tm = TM_PROJ
    nt = seq // tm
    nq, nk, g = NSA_HEADS * HEAD_DIM, NSA_GROUPS * HEAD_DIM, NSA_GROUPS
    seg = _head_blockdiag()
    rows = lambda n, dt: ((t, n), dt, (tm, n), lambda b, i: (b * nt + i, 0))
    keys = lambda n: ((batch, g, seq, n), BF16, (1, g, tm, n), lambda b, i: (b, 0, i, 0))
    vals = lambda r: ((batch, g, r, seq), BF16, (1, g, r, tm), lambda b, i: (b, 0, 0, i))
    slabs = 2 * nk // LANES
    outs = [rows(nq, BF16), rows(nq, BF16), keys(2 * LANES), keys(LANES),
            ((slabs, t, LANES), F32, (slabs, tm, LANES), lambda b, i: (0, b * nt + i, 0)),
            vals(HEAD_DIM + ONES_ROWS), vals(HEAD_DIM + ONES_ROWS), rows(LANES, F32)]
    return _proj_call(_nsa_proj_kernel, "nsa_proj", batch, seq, (x2, cosf, sinf),
                      (mod, gln, wn, wr, wg, bg, seg, gain), outs)


def _diff_proj(x2, mod, gln, wn, wr, gain, cosf, sinf, batch, seq):
    t = x2.shape[0]
    tm = TM_PROJ
    nt = seq // tm
    nq = wn.shape[1] // 2
    hw = 2 * HEAD_DIM
    heads = wr.shape[1] // hw
    seg = _head_blockdiag()
    outs = [((t, nq), BF16, (tm, nq), lambda b, i: (b * nt + i, 0)),
            ((batch, seq, nq), BF16, (1, tm, nq), lambda b, i: (b, i, 0)),
            ((batch, heads, hw + ONES_ROWS, seq), BF16, (1, heads, hw + ONES_ROWS, tm),
             lambda b, i: (b, 0, 0, i))]
    return _proj_call(_diff_proj_kernel, "diff_proj", batch, seq, (x2, cosf, sinf),
                      (mod, gln, wn, wr, seg, gain), outs)


def _compress_hidden(x_ref, pe_ref, w1_ref):
    nh = x_ref.shape[1] // CMP_STRIDE
    width = w1_ref.shape[2]

    def body(l, carry):
        a, b = carry
        x = x_ref[0, pl.ds(l, nh, stride=CMP_STRIDE), :]
        za = (x + pe_ref[pl.ds(l, 1), :]).astype(BF16)
        zb = (x + pe_ref[pl.ds(l + CMP_STRIDE, 1), :]).astype(BF16)
        return a + _dot(za, w1_ref[l]), b + _dot(zb, w1_ref[l + CMP_STRIDE])

    zero = jnp.zeros((nh, width), F32)
    a, b = lax.fori_loop(0, CMP_STRIDE, body, (zero, zero), unroll=4)
    pre = a + pltpu.roll(b, nh - 1, 0)
    return (pre * _sigmoid(pre)).astype(BF16)


def _compress_kernel(xk_ref, xv_ref, pek_ref, pev_ref, wk1_ref, wk2_ref, wv1_ref, wv2t_ref,
                     kg_ref, kc_ref, vct_ref):
    k = _dot(_compress_hidden(xk_ref, pek_ref, wk1_ref), wk2_ref[...])
    hv = _compress_hidden(xv_ref, pev_ref, wv1_ref)
    for g in range(2):
        kg = k[:, g * LANES:(g + 1) * LANES]
        ms = jnp.mean(kg * kg, axis=-1, keepdims=True)
        kc_ref[0, g] = (kg * lax.rsqrt(ms + EPS) * kg_ref[...]).astype(BF16)
        vct_ref[0, g] = _dot_nt(wv2t_ref[...], hv[:, g * CMP_HIDDEN:(g + 1) * CMP_HIDDEN]).astype(BF16)


def _nsa_compress(kcv, batch, seq, pek, pev, wk1, wk2, wv1, wv2t, kgdup):
    nh = seq // CMP_STRIDE
    pairs = NSA_GROUPS // 2
    full = lambda s: pl.BlockSpec(s, lambda i, j: (0,) * len(s))
    return pl.pallas_call(
        _compress_kernel,
        grid=(batch, pairs),
        in_specs=[pl.BlockSpec((1, seq, LANES), lambda i, j: (j, i, 0)),
                  pl.BlockSpec((1, seq, LANES), lambda i, j: (pairs + j, i, 0)),
                  full(pek.shape), full(pev.shape), full(wk1.shape), full(wk2.shape),
                  full(wv1.shape), full(wv2t.shape), full(kgdup.shape)],
        out_specs=[pl.BlockSpec((1, 2, nh, LANES), lambda i, j: (i, j, 0, 0)),
                   pl.BlockSpec((1, 2, HEAD_DIM, nh), lambda i, j: (i, j, 0, 0))],
        out_shape=[jax.ShapeDtypeStruct((batch, NSA_GROUPS, nh, LANES), BF16),
                   jax.ShapeDtypeStruct((batch, NSA_GROUPS, HEAD_DIM, nh), BF16)],
        compiler_params=pltpu.CompilerParams(
            dimension_semantics=("arbitrary", "arbitrary"), vmem_limit_bytes=VMEM_LIMIT),
        name="nsa_compress",
    )(kcv, kcv, pek, pev, wk1, wk2, wv1, wv2t, kgdup)


def _half_masks(rows):
    lane = lax.broadcasted_iota(jnp.int32, (rows, LANES), 1)
    return lane < HEAD_DIM, lane >= HEAD_DIM


def _flash_reset(m_ref, acc_ref):
    m_ref[...] = jnp.full(m_ref.shape, NEG_INF, F32)
    acc_ref[...] = jnp.zeros(acc_ref.shape, F32)


def _flash_produce(buf, u, st, with_max):
    s_ref, t_ref = buf
    s_ref[u] = st
    if with_max:
        t_ref[u] = jnp.max(st, axis=0, keepdims=True)


def _flash_consume(buf, u, mask, vt, m_ref, acc_ref):
    s_ref, t_ref = buf
    st = s_ref[u]
    if mask is None:
        tmax = t_ref[u]
    else:
        st = jnp.where(mask, st, NEG_INF)
        tmax = jnp.max(st, axis=0, keepdims=True)
    m_old = m_ref[u]
    m_new = jnp.maximum(m_old, tmax)
    p = jnp.exp2(st - m_new).astype(BF16)
    acc_ref[u] = jnp.exp2(m_old - m_new) * acc_ref[u] + _dot(vt, p)
    m_ref[u] = m_new


def _flash_pipeline(n_full, tk, n_str, score_fn, mask_fn, value_fn, bufs, m_ref, acc_ref):
    _flash_reset(m_ref, acc_ref)

    def produce(buf, k0):
        for u in range(n_str):
            _flash_produce(buf, u, score_fn(u, k0), True)

    def consume(buf, k0, causal):
        for u in range(n_str):
            _flash_consume(buf, u, mask_fn(k0) if causal else None, value_fn(u, k0),
                           m_ref, acc_ref)

    produce(bufs[0], 0)

    def step(dst, src, k_dst, k_src):
        for u in range(n_str):
            _flash_produce(dst, u, score_fn(u, k_dst), True)
            _flash_consume(src, u, None, value_fn(u, k_src), m_ref, acc_ref)

    def pair(jj, carry):
        k0 = pl.multiple_of(jj * (2 * tk), 2 * tk)
        step(bufs[1], bufs[0], k0 + tk, k0)
        step(bufs[0], bufs[1], k0 + 2 * tk, k0 + tk)
        return carry

    n_pairs = lax.shift_right_logical(n_full, 1)
    lax.fori_loop(0, n_pairs, pair, 0)
    k0 = pl.multiple_of(n_pairs * (2 * tk), 2 * tk)
    odd = (n_full & 1) == 1

    @pl.when(odd)
    def _():
        step(bufs[1], bufs[0], k0 + tk, k0)
        consume(bufs[1], k0 + tk, True)

    @pl.when(jnp.logical_not(odd))
    def _():
        consume(bufs[0], k0, True)


def _nsa_attn_kernel(qn_ref, qr_ref, gate_ref, kc_ref, vct_ref, ks_ref, vst_ref, kw_ref, vwt_ref,
                     ovt_ref, o_ref, gt_ref, val_ref, pc_ref, ow_ref, s0_ref, t0_ref, s1_ref, t1_ref,
                     m_ref, acc_ref,
                     *, tq, tk, gs, n_sel):
    g0 = pl.program_id(1) * gs
    i = pl.program_id(2)
    q0 = i * tq
    nqs = NSA_HPG * tq
    gw = NSA_HPG * HEAD_DIM
    ncb = kc_ref.shape[2]
    top_n = float(min(SEL_TOPK, n_sel))
    lo, hi = _half_masks(tq)
    tq_row = q0 + lax.broadcasted_iota(jnp.int32, (1, tq), 1)
    t4 = q0 + (lax.broadcasted_iota(jnp.int32, (1, nqs), 1) & (tq - 1))

    def stack_heads(ref, u):
        parts = []
        for j in range(NSA_HPG):
            c0 = u * gw + (j // 2) * LANES
            pair = ref[:, c0:c0 + LANES]
            parts.append(jnp.where(lo if j % 2 == 0 else hi, pair, jnp.zeros_like(pair)))
        return jnp.concatenate(parts, axis=0)

    qr_s = [stack_heads(qr_ref, u) for u in range(gs)]
    bufs = ((s0_ref, t0_ref), (s1_ref, t1_ref))

    def normalized(u):
        acc = acc_ref[u]
        return acc[0:HEAD_DIM] * (1.0 / acc[HEAD_DIM:HEAD_DIM + 1])

    krow = lax.broadcasted_iota(jnp.int32, (tk, nqs), 0)
    qcol = lax.broadcasted_iota(jnp.int32, (tk, nqs), 1) & (tq - 1)
    wtiles = ((jnp.maximum(q0 - 2 * tk, 0), krow > qcol + jnp.where(i >= 2, 0, tk)),
              (jnp.maximum(q0 - tk, 0), krow >= jnp.where(i >= 1, 0, tk)),
              (q0, krow <= qcol))

    def wproduce(buf, w, u):
        k0 = pl.multiple_of(wtiles[w][0], tk)
        _flash_produce(buf, u, _dot_nt(kw_ref[0, u, pl.ds(k0, tk), :], qr_s[u]), False)

    def wconsume(buf, w, u):
        k0 = pl.multiple_of(wtiles[w][0], tk)
        _flash_consume(buf, u, wtiles[w][1], vwt_ref[0, u, :, pl.ds(k0, tk)], m_ref, acc_ref)

    _flash_reset(m_ref, acc_ref)
    for u in range(gs):
        wproduce(bufs[0], 0, u)
    for w in range(1, len(wtiles)):
        for u in range(gs):
            wproduce(bufs[w % 2], w, u)
            wconsume(bufs[(w - 1) % 2], w - 1, u)
    for u in range(gs):
        wconsume(bufs[(len(wtiles) - 1) % 2], len(wtiles) - 1, u)
        ow_ref[u] = normalized(u)

    cend = lax.broadcasted_iota(jnp.int32, (ncb, nqs), 0) * CMP_STRIDE + (CMP_BLOCK - 1)
    cmask = cend <= t4
    any_blk = t4 >= CMP_BLOCK - 1
    blk = lax.broadcasted_iota(jnp.int32, (n_sel, tq), 0)
    bt = lax.shift_right_arithmetic(tq_row, int(math.log2(SEL_BLOCK)))
    forced = (blk == 0) | (blk == bt) | (blk == bt - 1)
    ovt = ovt_ref[...]
    o_cmp = []
    for u in range(gs):
        st = jnp.where(cmask, _dot_nt(kc_ref[0, u], stack_heads(qn_ref, u)), NEG_INF)
        pc_ref[...] = jnp.exp2(st - jnp.max(st, axis=0, keepdims=True))
        l = jnp.sum(pc_ref[...], axis=0, keepdims=True)
        inv = jnp.where(any_blk, 1.0 / l, 0.0)
        o_cmp.append(_dot(vct_ref[0, u], pc_ref[...].astype(BF16)) * inv)
        pc_sum = pc_ref[:, 0:tq] * inv[:, 0:tq]
        for j in range(1, NSA_HPG):
            cols = slice(j * tq, (j + 1) * tq)
            pc_sum = pc_sum + pc_ref[:, cols] * inv[:, cols]
        imp = sum(_dot(ovt, part) for part in _split3(pc_sum))
        val_ref[u] = jnp.where(forced, SEL_FORCE, jnp.where(blk <= bt, imp[0:n_sel, :], -1.0))

    per_it = tq // SEL_BLOCK
    vals = [val_ref[u] for u in range(gs)]

    def rank_body(it, ranks):
        ranks = list(ranks)
        for r in range(per_it):
            jp = it * per_it + r
            tie = jnp.where(blk > jp, 1.0, 0.0)
            for u in range(gs):
                row = val_ref[u, pl.ds(jp, 1), :]
                ranks[u] = ranks[u] + jnp.where(row > vals[u], 1.0,
                                                jnp.where(row == vals[u], tie, 0.0))
        return tuple(ranks)

    ranks = lax.fori_loop(0, i + 1, rank_body,
                          tuple(jnp.zeros((n_sel, tq), F32) for _ in range(gs)))

    q_aug = []
    for u in range(gs):
        selb = jnp.where(ranks[u] < top_n, 0.0, NEG_INF)
        selb = jnp.concatenate([selb, jnp.zeros((LANES - n_sel, tq), F32)], axis=0).T
        selb = selb.astype(BF16)
        q_aug.append(jnp.concatenate(
            [qr_s[u], jnp.concatenate([selb] * NSA_HPG, axis=0)], axis=1))

    _flash_pipeline(
        q0 // tk, tk, gs,
        lambda u, k0: _dot_nt(ks_ref[0, u, pl.ds(k0, tk), :], q_aug[u]),
        lambda k0: k0 + lax.broadcasted_iota(jnp.int32, (tk, nqs), 0) <= t4,
        lambda u, k0: vst_ref[0, u, :, pl.ds(k0, tk)],
        ((s0_ref, t0_ref), (s1_ref, t1_ref)), m_ref, acc_ref)

    gt_ref[...] = gate_ref[...].T
    for u in range(gs):
        o_sel = normalized(u)
        o_win = ow_ref[u]

        def gate_rows(branch):
            base = branch * NSA_HEADS + (g0 + u) * NSA_HPG
            return jnp.concatenate(
                [gt_ref[pl.ds(base + j, 1), :] for j in range(NSA_HPG)], axis=1)

        o = gate_rows(0) * o_cmp[u] + gate_rows(1) * o_sel + gate_rows(2) * o_win
        o = jnp.concatenate([o[:, j * tq:(j + 1) * tq] for j in range(NSA_HPG)], axis=0)
        o_ref[:, u * gw:(u + 1) * gw] = o.T.astype(BF16)


def _nsa_attn(qn, qr, gates, kc_dup, vct, ks_aug, vst, kw_dup, vwt, ovt):
    tq, tk, gs = TQ_NSA, TK_NSA, GS_NSA
    b, g, s, _ = ks_aug.shape
    t = qn.shape[0]
    gw = NSA_HPG * HEAD_DIM
    nq = s // tq
    n_sel = s // SEL_BLOCK
    nqs = NSA_HPG * tq
    qspec = pl.BlockSpec((tq, gs * gw), lambda bi, gi, i: (bi * nq + i, gi))
    per_bg = lambda a: pl.BlockSpec((1, gs) + a.shape[2:], lambda bi, gi, i: (bi, gi, 0, 0))
    return pl.pallas_call(
        functools.partial(_nsa_attn_kernel, tq=tq, tk=tk, gs=gs, n_sel=n_sel),
        grid=(b, g // gs, nq),
        in_specs=[qspec, qspec,
                  pl.BlockSpec((tq, LANES), lambda bi, gi, i: (bi * nq + i, 0)),
                  per_bg(kc_dup), per_bg(vct), per_bg(ks_aug), per_bg(vst), per_bg(kw_dup),
                  per_bg(vwt), pl.BlockSpec(ovt.shape, lambda bi, gi, i: (0, 0))],
        out_specs=qspec,
        out_shape=jax.ShapeDtypeStruct((t, g * gw), BF16),
        scratch_shapes=[pltpu.VMEM((LANES, tq), F32),
                        pltpu.VMEM((gs, n_sel, tq), F32),
                        pltpu.VMEM((kc_dup.shape[2], nqs), F32),
                        pltpu.VMEM((gs, HEAD_DIM, nqs), F32),
                        pltpu.VMEM((gs, tk, nqs), F32), pltpu.VMEM((gs, 1, nqs), F32),
                        pltpu.VMEM((gs, tk, nqs), F32), pltpu.VMEM((gs, 1, nqs), F32),
                        pltpu.VMEM((gs, 1, nqs), F32),
                        pltpu.VMEM((gs, vst.shape[2], nqs), F32)],
        compiler_params=pltpu.CompilerParams(
            dimension_semantics=("arbitrary", "arbitrary", "arbitrary"),
            vmem_limit_bytes=VMEM_LIMIT),
        name="nsa_attn",
    )(qn, qr, gates, kc_dup, vct, ks_aug, vst, kw_dup, vwt, ovt)


def _diff_attn_kernel(lamv_ref, q_ref, k_ref, vt_ref, sg_ref, o_ref, s0_ref, t0_ref, s1_ref, t1_ref,
                      m_ref, acc_ref, *, tq, hs, lam_init):
    lv = lamv_ref[...]
    lam = (jnp.exp(jnp.sum(lv[0:1] * lv[1:2], axis=-1, keepdims=True))
           - jnp.exp(jnp.sum(lv[2:3] * lv[3:4], axis=-1, keepdims=True)) + lam_init)
    hw = 2 * HEAD_DIM
    lo, hi = _half_masks(tq)
    q_s = []
    for u in range(hs):
        q = q_ref[:, u * hw:(u + 1) * hw]
        zero = jnp.zeros_like(q)
        q_s.append(jnp.concatenate([jnp.where(lo, q, zero), jnp.where(hi, q, zero)], axis=0))

    def diag_mask(k0):
        kpos = lax.broadcasted_iota(jnp.int32, (tq, 2 * tq), 0)
        qpos = lax.broadcasted_iota(jnp.int32, (tq, 2 * tq), 1) & (tq - 1)
        return kpos <= qpos

    _flash_pipeline(
        pl.program_id(2), tq, hs,
        lambda u, k0: _dot_nt(k_ref[0, pl.ds(k0, tq), u * hw:(u + 1) * hw], q_s[u]),
        diag_mask,
        lambda u, k0: vt_ref[0, u, :, pl.ds(k0, tq)],
        ((s0_ref, t0_ref), (s1_ref, t1_ref)), m_ref, acc_ref)

    for u in range(hs):
        acc = acc_ref[u]
        o = acc[0:hw] * (1.0 / acc[hw:hw + 1])
        o = (o[:, 0:tq] - lam * o[:, tq:2 * tq]).T
        ms = jnp.mean(o * o, axis=-1, keepdims=True)
        o_ref[:, u * hw:(u + 1) * hw] = (
            (o * lax.rsqrt(ms + EPS) * sg_ref[...]) * (1.0 - lam_init)).astype(BF16)


def _diff_attn(lamv, q, k, vt, sg, lam_init):
    tq, hs = TQ_DIFF, HS_DIFF
    b, s, w = k.shape
    hw = 2 * HEAD_DIM
    heads = w // hw
    vw = vt.shape[2]
    nq = s // tq
    full = lambda shp: pl.BlockSpec(shp, lambda bi, hi, i: (0,) * len(shp))
    qspec = pl.BlockSpec((tq, hs * hw), lambda bi, hi, i: (bi * nq + i, hi))
    return pl.pallas_call(
        functools.partial(_diff_attn_kernel, tq=tq, hs=hs, lam_init=lam_init),
        grid=(b, heads // hs, nq),
        in_specs=[full(lamv.shape), qspec,
                  pl.BlockSpec((1, s, hs * hw), lambda bi, hi, i: (bi, 0, hi)),
                  pl.BlockSpec((1, hs, vw, s), lambda bi, hi, i: (bi, hi, 0, 0)),
                  full(sg.shape)],
        out_specs=qspec,
        out_shape=jax.ShapeDtypeStruct(q.shape, BF16),
        scratch_shapes=[pltpu.VMEM((hs, tq, 2 * tq), F32), pltpu.VMEM((hs, 1, 2 * tq), F32),
                        pltpu.VMEM((hs, tq, 2 * tq), F32), pltpu.VMEM((hs, 1, 2 * tq), F32),
                        pltpu.VMEM((hs, 1, 2 * tq), F32), pltpu.VMEM((hs, vw, 2 * tq), F32)],
        compiler_params=pltpu.CompilerParams(
            dimension_semantics=("arbitrary", "arbitrary", "arbitrary"),
            vmem_limit_bytes=VMEM_LIMIT),
        name="diff_attn",
    )(lamv, q, k, vt, sg)


def _mlp_kernel(x_ref, a_ref, mod_ref, gln_ref, wo_ref, w1_ref, w2_ref, o_ref,
                x1_ref, h_ref, acc_ref):
    kf = pl.program_id(1)
    m = mod_ref[0]

    @pl.when(kf == 0)
    def _():
        chunk = x_ref.shape[0] // MLP_ROW_CHUNKS
        rows = [slice(r * chunk, (r + 1) * chunk) for r in range(MLP_ROW_CHUNKS)]
        proj = [_dot(a_ref[r, :], wo_ref[...]) for r in rows]
        for r, pr in zip(rows, proj):
            x1 = x_ref[r, :] + m[2:3] * pr
            x1_ref[r, :] = x1
            h_ref[r, :] = _mod_rmsnorm(x1, gln_ref[...], m[3:4], m[4:5]).astype(BF16)
        acc_ref[...] = jnp.zeros(acc_ref.shape, F32)

    u = jnp.maximum(_dot(h_ref[...], w1_ref[...]), 0.0)
    acc_ref[...] += _dot((u * u).astype(BF16), w2_ref[...])

    @pl.when(kf == pl.num_programs(1) - 1)
    def _():
        o_ref[...] = x1_ref[...] + m[5:6] * acc_ref[...]


def _outproj_mlp(x2, attn, mod, gln, wo, w1, w2, seq):
    tm, tf = TM_MLP, TF_MLP
    t, d = x2.shape
    dff = w1.shape[1]
    return pl.pallas_call(
        _mlp_kernel,
        grid=(t // tm, dff // tf),
        in_specs=[
            pl.BlockSpec((tm, d), lambda i, k: (i, 0)),
            pl.BlockSpec((tm, d), lambda i, k: (i, 0)),
            pl.BlockSpec((1, 8, d), lambda i, k: ((i * tm) // seq, 0, 0)),
            pl.BlockSpec((1, d), lambda i, k: (0, 0)),
            pl.BlockSpec((d, d), lambda i, k: (0, 0)),
            pl.BlockSpec((d, tf), lambda i, k: (0, k)),
            pl.BlockSpec((tf, d), lambda i, k: (k, 0)),
        ],
        out_specs=pl.BlockSpec((tm, d), lambda i, k: (i, 0)),
        out_shape=jax.ShapeDtypeStruct((t, d), F32),
        scratch_shapes=[pltpu.VMEM((tm, d), F32), pltpu.VMEM((tm, d), BF16),
                        pltpu.VMEM((tm, d), F32)],
        compiler_params=pltpu.CompilerParams(
            dimension_semantics=("arbitrary", "arbitrary"), vmem_limit_bytes=VMEM_LIMIT),
        name="outproj_mlp",
    )(x2, attn, mod, gln, wo, w1, w2)


def _rope_lane_tables(positions):
    inv = ROPE_THETA ** (-jnp.arange(0, ROPE_DIM, 2, dtype=F32) / ROPE_DIM)
    ang = positions.astype(F32)[..., None] * inv
    cos, sin = jnp.cos(ang), jnp.sin(ang)
    rest = HEAD_DIM - ROPE_DIM
    ones = jnp.ones(cos.shape[:-1] + (rest,), F32)
    zeros = jnp.zeros(cos.shape[:-1] + (rest,), F32)
    cosh = jnp.concatenate([cos, cos, ones], axis=-1)
    sinh = jnp.concatenate([-sin, sin, zeros], axis=-1)
    reps = LANES // HEAD_DIM
    t = positions.shape[0] * positions.shape[1]
    return (jnp.tile(cosh, (1, 1, reps)).reshape(t, LANES),
            jnp.tile(sinh, (1, 1, reps)).reshape(t, LANES))


def _tile_gain(g, heads):
    return jnp.tile(g.astype(F32), heads)


def _dup_heads(w):
    h = w.reshape(w.shape[:-1] + (-1, 1, HEAD_DIM))
    return jnp.concatenate([h, h], axis=-2).reshape(w.shape[:-1] + (-1,))


def kernel(x, c, positions, ln_mix_g, ln_mlp_g, w_ada, b_ada, w_mlp_in, w_mlp_out, nsa_w_in, nsa_b_gate, nsa_q_gain, nsa_k_gain, nsa_pe_k, nsa_w_ck1, nsa_w_ck2, nsa_pe_v, nsa_w_cv1, nsa_w_cv2, nsa_w_out, diff_w_in, diff_q_gain, diff_k_gain, diff_lq1, diff_lk1, diff_lq2, diff_lk2, diff_subln_g, diff_w_out):
    b, s, d = x.shape
    depth = w_ada.shape[0]
    t = b * s
    n_sel = s // SEL_BLOCK
    nq, nk = NSA_HEADS * HEAD_DIM, NSA_GROUPS * HEAD_DIM
    assert s % TM_PROJ == 0 and s % TM_MLP == 0 and s % (2 * TK_NSA) == 0
    assert n_sel <= LANES and TK_NSA == TQ_NSA and WINDOW == 2 * TK_NSA

    cosf, sinf = _rope_lane_tables(positions)
    c_pad = jnp.zeros((8, d), F32).at[:b].set(c)
    mod = _ada_mod(c_pad, w_ada, b_ada)
    mod = mod[:, :b].reshape(depth, b, 6, d)
    mod = jnp.pad(mod, ((0, 0), (0, 0), (0, 2), (0, 0)))

    n_half = s // CMP_STRIDE
    cb = jnp.arange(n_half)
    sb = jnp.arange(LANES)
    ovt = ((cb[None, :] * CMP_STRIDE <= sb[:, None] * SEL_BLOCK + SEL_BLOCK - 1)
           & (cb[None, :] * CMP_STRIDE + CMP_BLOCK - 1 >= sb[:, None] * SEL_BLOCK)
           & (sb[:, None] < n_sel)).astype(BF16)

    x2 = x.reshape(t, d)
    for i in range(depth):
        j = i // 2
        gln = ln_mix_g[i].reshape(1, d)
        if i % 2 == 0:
            w = nsa_w_in[j]
            o0 = nq
            wq = w[:, :o0]
            wkc, wvc, wks, wvs, wkw, wvw = [w[:, o0 + r * nk:o0 + (r + 1) * nk] for r in range(6)]
            wgl = w[:, o0 + 6 * nk:]
            wn = jnp.concatenate([wq, wks, wkw], axis=1).astype(BF16)
            wr = jnp.concatenate([wkc, wvc, wvs, wvw], axis=1).astype(BF16)
            wg = wgl.reshape(d, NSA_HEADS, 3).transpose(0, 2, 1).reshape(d, 3 * NSA_HEADS)
            wg = jnp.pad(wg, ((0, 0), (0, LANES - 3 * NSA_HEADS))).astype(BF16)
            bg = nsa_b_gate[j].reshape(NSA_HEADS, 3).T.reshape(1, 3 * NSA_HEADS)
            bg = jnp.pad(bg, ((0, 0), (0, LANES - 3 * NSA_HEADS)))
            gain = jnp.concatenate([_tile_gain(nsa_q_gain[j], NSA_HEADS),
                                    _tile_gain(nsa_k_gain[j, 1], NSA_GROUPS),
                                    _tile_gain(nsa_k_gain[j, 2], NSA_GROUPS)]).reshape(1, -1)
            qn, qr, ks_aug, kw_dup, kcv, vst, vwt, gates = _nsa_proj(
                x2, mod[i], gln, wn, wr, wg, bg, gain, cosf, sinf, b, s)

            def pair_blockdiag(w):
                z = jnp.zeros_like(w)
                return jnp.concatenate([jnp.concatenate([w, z], axis=-1),
                                        jnp.concatenate([z, w], axis=-1)], axis=-2)

            def w1_stack(w1):
                return pair_blockdiag(w1.reshape(CMP_BLOCK, HEAD_DIM, CMP_HIDDEN)).astype(BF16)

            kc_dup, vct = _nsa_compress(
                kcv, b, s, jnp.tile(nsa_pe_k[j], (1, 2)), jnp.tile(nsa_pe_v[j], (1, 2)),
                w1_stack(nsa_w_ck1[j]), pair_blockdiag(_dup_heads(nsa_w_ck2[j])).astype(BF16),
                w1_stack(nsa_w_cv1[j]), nsa_w_cv2[j].T.astype(BF16),
                _tile_gain(nsa_k_gain[j, 0], 2).reshape(1, 2 * HEAD_DIM))
            attn = _nsa_attn(qn, qr, gates, kc_dup, vct, ks_aug, vst, kw_dup, vwt, ovt)
            wo = nsa_w_out[j]
        else:
            w = diff_w_in[j]
            dw = w.shape[1] // 3
            heads = dw // HEAD_DIM
            wn = w[:, :2 * dw].astype(BF16)
            wr = w[:, 2 * dw:].astype(BF16)
            gain = jnp.concatenate([_tile_gain(diff_q_gain[j], heads),
                                    _tile_gain(diff_k_gain[j], heads)]).reshape(1, -1)
            q, k, vt = _diff_proj(x2, mod[i], gln, wn, wr, gain, cosf, sinf, b, s)
            lamv = jnp.stack([diff_lq1[j], diff_lk1[j], diff_lq2[j], diff_lk2[j]]).astype(F32)
            lamv = jnp.pad(lamv, ((0, 4), (0, LANES - HEAD_DIM)))
            lam_init = 0.8 - 0.6 * math.exp(-0.3 * i)
            attn = _diff_attn(lamv, q, k, vt, diff_subln_g[j].reshape(1, -1), lam_init)
            wo = diff_w_out[j]
        x2 = _outproj_mlp(x2, attn, mod[i], ln_mlp_g[i].reshape(1, d), wo.astype(BF16),
                          w_mlp_in[i].astype(BF16), w_mlp_out[i].astype(BF16), s)
    return x2.reshape(b, s, d)
```

```python
import functools
import math

import jax
import jax.numpy as jnp
from jax import lax
from jax.experimental import pallas as pl
from jax.experimental.pallas import tpu as pltpu

F32 = jnp.float32
BF16 = jnp.bfloat16

LANES = 128
HEAD_DIM = 64
ROPE_DIM = HEAD_DIM // 4
ROPE_THETA = 500000.0
NSA_GROUPS = 4
NSA_HPG = 4
NSA_HEADS = NSA_GROUPS * NSA_HPG
CMP_BLOCK = 32
CMP_STRIDE = 16
CMP_HIDDEN = 2 * HEAD_DIM
SEL_BLOCK = 64
SEL_TOPK = 16
WINDOW = 512
EPS = 1e-6
NEG_INF = -1e30
SEL_FORCE = 1e6
QK_SCALE = math.log2(math.e) / math.sqrt(HEAD_DIM)
ONES_ROWS = 16

VMEM_LIMIT = 48 * 1024 * 1024
TM_PROJ = 512
TM_MLP, TF_MLP = 512, 2048
MLP_ROW_CHUNKS = 4
TQ_NSA, TK_NSA, GS_NSA = 256, 256, 2
TQ_DIFF, HS_DIFF = 256, 4
SCORE_DTYPE = BF16


def _dot(a, b):
    return jnp.dot(a, b, preferred_element_type=F32)


def _dot_nt(a, b):
    return lax.dot_general(a, b, (((1,), (1,)), ((), ())), preferred_element_type=F32)


def _sigmoid(x):
    return 1.0 / (1.0 + jnp.exp(-x))


def _split3(x):
    hi = x.astype(BF16)
    r1 = x - hi.astype(F32)
    mid = r1.astype(BF16)
    lo = (r1 - mid.astype(F32)).astype(BF16)
    return hi, mid, lo


def _mod_rmsnorm(x, g_ln, shift, scale):
    ms = jnp.mean(x * x, axis=-1, keepdims=True)
    return (x * lax.rsqrt(ms + EPS) * g_ln) * (1.0 + scale) + shift


def _head_rmsnorm(a, seg_ref, gain):
    sq = (a * a).astype(BF16)
    seg = seg_ref[...]
    w = seg.shape[0]
    ss = jnp.concatenate(
        [_dot(sq[:, j * w:(j + 1) * w], seg) for j in range(a.shape[1] // w)], axis=1)
    return a * lax.rsqrt(ss * (1.0 / HEAD_DIM) + EPS) * gain


def _rope(a, cosf, sinf):
    lane = lax.broadcasted_iota(jnp.int32, (a.shape[0], LANES), 1)
    first = (lane & (HEAD_DIM - 1)) < (ROPE_DIM // 2)
    half = ROPE_DIM // 2
    outs = []
    for j in range(a.shape[1] // LANES):
        blk = a[:, j * LANES:(j + 1) * LANES]
        partner = jnp.where(first, pltpu.roll(blk, LANES - half, 1), pltpu.roll(blk, half, 1))
        outs.append(blk * cosf + partner * sinf)
    return jnp.concatenate(outs, axis=1)


def _ada_kernel(c_ref, w_ref, b_ref, o_ref):
    c = c_ref[...]
    cond = c * _sigmoid(c)
    w = w_ref[0]
    c_hi = cond.astype(BF16)
    c_lo = (cond - c_hi.astype(F32)).astype(BF16)
    w_hi = w.astype(BF16)
    w_lo = (w - w_hi.astype(F32)).astype(BF16)
    o_ref[0] = _dot(c_hi, w_hi) + _dot(c_lo, w_hi) + _dot(c_hi, w_lo) + b_ref[0]


def _ada_mod(c_pad, w_ada, b_ada):
    depth, d, n = w_ada.shape
    tn = 1536
    return pl.pallas_call(
        _ada_kernel,
        grid=(depth, n // tn),
        in_specs=[
            pl.BlockSpec((8, d), lambda l, j: (0, 0)),
            pl.BlockSpec((1, d, tn), lambda l, j: (l, 0, j)),
            pl.BlockSpec((1, 1, tn), lambda l, j: (l, 0, j)),
        ],
        out_specs=pl.BlockSpec((1, 8, tn), lambda l, j: (l, 0, j)),
        out_shape=jax.ShapeDtypeStruct((depth, 8, n), F32),
        compiler_params=pltpu.CompilerParams(
            dimension_semantics=("arbitrary", "arbitrary"), vmem_limit_bytes=VMEM_LIMIT),
        name="ada_mod",
    )(c_pad, w_ada, b_ada.reshape(depth, 1, n))


def _store_vt(vt_ref, v, dv):
    step = max(dv, LANES)
    for c in range(v.shape[1] // step):
        vt = v[:, c * step:(c + 1) * step].T
        for r in range(step // dv):
            j = c * (step // dv) + r
            vt_ref[0, j, 0:dv, :] = vt[r * dv:(r + 1) * dv, :].astype(BF16)
            if vt_ref.shape[2] > dv:
                vt_ref[0, j, dv:, :] = jnp.ones((vt_ref.shape[2] - dv, v.shape[0]), BF16)


def _nsa_proj_kernel(x_ref, mod_ref, gln_ref, wn_ref, wr_ref, wg_ref, bg_ref, seg_ref,
                     gain_ref, cos_ref, sin_ref,
                     qn_ref, qr_ref, ks_ref, kw_ref, kcv_ref, vst_ref, vwt_ref, gate_ref):
    tm = x_ref.shape[0]
    m = mod_ref[0]
    h = _mod_rmsnorm(x_ref[...], gln_ref[...], m[0:1], m[1:2]).astype(BF16)
    a = _dot(h, wn_ref[...])
    an = _head_rmsnorm(a, seg_ref, gain_ref[...])
    ar = _rope(an, cos_ref[...], sin_ref[...])
    nq = qn_ref.shape[1]
    nk = NSA_GROUPS * HEAD_DIM
    qn_ref[...] = (an[:, :nq] * QK_SCALE).astype(BF16)
    qr_ref[...] = (ar[:, :nq] * QK_SCALE).astype(BF16)
    pos = pl.program_id(1) * tm + lax.broadcasted_iota(jnp.int32, (tm, LANES), 0)
    lane = lax.broadcasted_iota(jnp.int32, (tm, LANES), 1)
    lo = lane < HEAD_DIM
    blk_onehot = jnp.where(
        lax.shift_right_logical(pos, int(math.log2(SEL_BLOCK))) == lane, 1.0, 0.0).astype(BF16)
    for c in range(nk // LANES):
        for src, dst, tail in ((nq, ks_ref, blk_onehot), (nq + nk, kw_ref, None)):
            x = ar[:, src + c * LANES:src + (c + 1) * LANES]
            xr = pltpu.roll(x, HEAD_DIM, 1)
            dst[0, 2 * c, :, 0:LANES] = jnp.where(lo, x, xr).astype(BF16)
            dst[0, 2 * c + 1, :, 0:LANES] = jnp.where(lo, xr, x).astype(BF16)
            if tail is not None:
                dst[0, 2 * c, :, LANES:2 * LANES] = tail
                dst[0, 2 * c + 1, :, LANES:2 * LANES] = tail
    raw = _dot(h, wr_ref[...])
    for c in range(kcv_ref.shape[0]):
        kcv_ref[c] = raw[:, c * LANES:(c + 1) * LANES]
    _store_vt(vst_ref, raw[:, 2 * nk:3 * nk], HEAD_DIM)
    _store_vt(vwt_ref, raw[:, 3 * nk:4 * nk], HEAD_DIM)
    gate_ref[...] = _sigmoid(_dot(h, wg_ref[...]) + bg_ref[...])


def _diff_proj_kernel(x_ref, mod_ref, gln_ref, wn_ref, wr_ref, seg_ref, gain_ref,
                      cos_ref, sin_ref, q_ref, k_ref, vt_ref):
    m = mod_ref[0]
    h = _mod_rmsnorm(x_ref[...], gln_ref[...], m[0:1], m[1:2]).astype(BF16)
    a = _dot(h, wn_ref[...])
    ar = _rope(_head_rmsnorm(a, seg_ref, gain_ref[...]), cos_ref[...], sin_ref[...])
    nq = q_ref.shape[1]
    q_ref[...] = (ar[:, :nq] * QK_SCALE).astype(BF16)
    k_ref[0] = ar[:, nq:].astype(BF16)
    _store_vt(vt_ref, _dot(h, wr_ref[...]), 2 * HEAD_DIM)


def _head_blockdiag():
    head = jnp.arange(2 * LANES) // HEAD_DIM
    return (head[:, None] == head[None, :]).astype(BF16)


def _proj_call(kernel, name, batch, seq, row_inputs, full_inputs, outs):
    tm = TM_PROJ
    nt = seq // tm
    rows = lambda n: pl.BlockSpec((tm, n), lambda b, i: (b * nt + i, 0))
    full = lambda shp: pl.BlockSpec(shp, lambda b, i: (0,) * len(shp))
    x2, cosf, sinf = row_inputs
    mod = full_inputs[0]
    in_specs = ([rows(x2.shape[1]), pl.BlockSpec((1,) + mod.shape[1:], lambda b, i: (b, 0, 0))]
                + [full(a.shape) for a in full_inputs[1:]]
                + [rows(cosf.shape[1]), rows(sinf.shape[1])])
    return pl.pallas_call(
        kernel,
        grid=(batch, nt),
        in_specs=in_specs,
        out_specs=[pl.BlockSpec(blk, imap) for (_, _, blk, imap) in outs],
        out_shape=[jax.ShapeDtypeStruct(shp, dt) for (shp, dt, _, _) in outs],
        compiler_params=pltpu.CompilerParams(
            dimension_semantics=("arbitrary", "arbitrary"), vmem_limit_bytes=VMEM_LIMIT),
        name=name,
    )(x2, *full_inputs, cosf, sinf)


def _nsa_proj(x2, mod, gln, wn, wr, wg, bg, gain, cosf, sinf, batch, seq):
    t = x2.shape[0]
    tm = TM_PROJ
    nt = seq // tm
    nq, nk, g = NSA_HEADS * HEAD_DIM, NSA_GROUPS * HEAD_DIM, NSA_GROUPS
    seg = _head_blockdiag()
    rows = lambda n, dt: ((t, n), dt, (tm, n), lambda b, i: (b * nt + i, 0))
    keys = lambda n: ((batch, g, seq, n), BF16, (1, g, tm, n), lambda b, i: (b, 0, i, 0))
    vals = lambda r: ((batch, g, r, seq), BF16, (1, g, r, tm), lambda b, i: (b, 0, 0, i))
    slabs = 2 * nk // LANES
    outs = [rows(nq, BF16), rows(nq, BF16), keys(2 * LANES), keys(LANES),
            ((slabs, t, LANES), F32, (slabs, tm, LANES), lambda b, i: (0, b * nt + i, 0)),
            vals(HEAD_DIM + ONES_ROWS), vals(HEAD_DIM + ONES_ROWS), rows(LANES, F32)]
    return _proj_call(_nsa_proj_kernel, "nsa_proj", batch, seq, (x2, cosf, sinf),
                      (mod, gln, wn, wr, wg, bg, seg, gain), outs)


def _diff_proj(x2, mod, gln, wn, wr, gain, cosf, sinf, batch, seq):
    t = x2.shape[0]
    tm = TM_PROJ
    nt = seq // tm
    nq = wn.shape[1] // 2
    hw = 2 * HEAD_DIM
    heads = wr.shape[1] // hw
    seg = _head_blockdiag()
    outs = [((t, nq), BF16, (tm, nq), lambda b, i: (b * nt + i, 0)),
            ((batch, seq, nq), BF16, (1, tm, nq), lambda b, i: (b, i, 0)),
            ((batch, heads, hw + ONES_ROWS, seq), BF16, (1, heads, hw + ONES_ROWS, tm),
             lambda b, i: (b, 0, 0, i))]
    return _proj_call(_diff_proj_kernel, "diff_proj", batch, seq, (x2, cosf, sinf),
                      (mod, gln, wn, wr, seg, gain), outs)


def _compress_hidden(x_ref, pe_ref, w1_ref):
    nh = x_ref.shape[1] // CMP_STRIDE
    width = w1_ref.shape[2]

    def body(l, carry):
        a, b = carry
        x = x_ref[0, pl.ds(l, nh, stride=CMP_STRIDE), :]
        za = (x + pe_ref[pl.ds(l, 1), :]).astype(BF16)
        zb = (x + pe_ref[pl.ds(l + CMP_STRIDE, 1), :]).astype(BF16)
        return a + _dot(za, w1_ref[l]), b + _dot(zb, w1_ref[l + CMP_STRIDE])

    zero = jnp.zeros((nh, width), F32)
    a, b = lax.fori_loop(0, CMP_STRIDE, body, (zero, zero), unroll=4)
    pre = a + pltpu.roll(b, nh - 1, 0)
    return (pre * _sigmoid(pre)).astype(BF16)


def _compress_kernel(xk_ref, xv_ref, pek_ref, pev_ref, wk1_ref, wk2_ref, wv1_ref, wv2t_ref,
                     kg_ref, kc_ref, vct_ref):
    k = _dot(_compress_hidden(xk_ref, pek_ref, wk1_ref), wk2_ref[...])
    hv = _compress_hidden(xv_ref, pev_ref, wv1_ref)
    for g in range(2):
        kg = k[:, g * LANES:(g + 1) * LANES]
        ms = jnp.mean(kg * kg, axis=-1, keepdims=True)
        kc_ref[0, g] = (kg * lax.rsqrt(ms + EPS) * kg_ref[...]).astype(BF16)
        vct_ref[0, g] = _dot_nt(wv2t_ref[...], hv[:, g * CMP_HIDDEN:(g + 1) * CMP_HIDDEN]).astype(BF16)


def _nsa_compress(kcv, batch, seq, pek, pev, wk1, wk2, wv1, wv2t, kgdup):
    nh = seq // CMP_STRIDE
    pairs = NSA_GROUPS // 2
    full = lambda s: pl.BlockSpec(s, lambda i, j: (0,) * len(s))
    return pl.pallas_call(
        _compress_kernel,
        grid=(batch, pairs),
        in_specs=[pl.BlockSpec((1, seq, LANES), lambda i, j: (j, i, 0)),
                  pl.BlockSpec((1, seq, LANES), lambda i, j: (pairs + j, i, 0)),
                  full(pek.shape), full(pev.shape), full(wk1.shape), full(wk2.shape),
                  full(wv1.shape), full(wv2t.shape), full(kgdup.shape)],
        out_specs=[pl.BlockSpec((1, 2, nh, LANES), lambda i, j: (i, j, 0, 0)),
                   pl.BlockSpec((1, 2, HEAD_DIM, nh), lambda i, j: (i, j, 0, 0))],
        out_shape=[jax.ShapeDtypeStruct((batch, NSA_GROUPS, nh, LANES), BF16),
                   jax.ShapeDtypeStruct((batch, NSA_GROUPS, HEAD_DIM, nh), BF16)],
        compiler_params=pltpu.CompilerParams(
            dimension_semantics=("arbitrary", "arbitrary"), vmem_limit_bytes=VMEM_LIMIT),
        name="nsa_compress",
    )(kcv, kcv, pek, pev, wk1, wk2, wv1, wv2t, kgdup)


def _half_masks(rows):
    lane = lax.broadcasted_iota(jnp.int32, (rows, LANES), 1)
    return lane < HEAD_DIM, lane >= HEAD_DIM


def _flash_reset(m_ref, acc_ref):
    m_ref[...] = jnp.full(m_ref.shape, NEG_INF, SCORE_DTYPE).astype(F32)
    acc_ref[...] = jnp.zeros(acc_ref.shape, F32)


def _flash_produce(buf, u, st, with_max):
    s_ref, t_ref = buf
    s_ref[u] = st.astype(s_ref.dtype)
    if with_max:
        t_ref[u] = jnp.max(st, axis=0, keepdims=True)


def _flash_consume(buf, u, mask, vt, m_ref, acc_ref):
    s_ref, t_ref = buf
    st = s_ref[u]
    if mask is None:
        tmax = t_ref[u]
    else:
        st = jnp.where(mask, st, jnp.asarray(NEG_INF, st.dtype))
        tmax = jnp.max(st, axis=0, keepdims=True).astype(F32)
    m_old = m_ref[u]
    m_new = jnp.maximum(m_old, tmax).astype(st.dtype)
    p = jnp.exp2(st - m_new).astype(BF16)
    m_new = m_new.astype(F32)
    acc_ref[u] = jnp.exp2(m_old - m_new) * acc_ref[u] + _dot(vt, p)
    m_ref[u] = m_new


def _flash_pipeline(n_full, tk, n_str, score_fn, mask_fn, value_fn, bufs, m_ref, acc_ref):
    _flash_reset(m_ref, acc_ref)

    def produce(buf, k0):
        for u in range(n_str):
            _flash_produce(buf, u, score_fn(u, k0), True)

    def consume(buf, k0, causal):
        for u in range(n_str):
            _flash_consume(buf, u, mask_fn(k0) if causal else None, value_fn(u, k0),
                           m_ref, acc_ref)

    produce(bufs[0], 0)

    def step(dst, src, k_dst, k_src):
        for u in range(n_str):
            _flash_produce(dst, u, score_fn(u, k_dst), True)
            _flash_consume(src, u, None, value_fn(u, k_src), m_ref, acc_ref)

    def pair(jj, carry):
        k0 = pl.multiple_of(jj * (2 * tk), 2 * tk)
        step(bufs[1], bufs[0], k0 + tk, k0)
        step(bufs[0], bufs[1], k0 + 2 * tk, k0 + tk)
        return carry

    n_pairs = lax.shift_right_logical(n_full, 1)
    lax.fori_loop(0, n_pairs, pair, 0)
    k0 = pl.multiple_of(n_pairs * (2 * tk), 2 * tk)
    odd = (n_full & 1) == 1

    @pl.when(odd)
    def _():
        step(bufs[1], bufs[0], k0 + tk, k0)
        consume(bufs[1], k0 + tk, True)

    @pl.when(jnp.logical_not(odd))
    def _():
        consume(bufs[0], k0, True)


def _nsa_attn_kernel(qn_ref, qr_ref, gate_ref, kc_ref, vct_ref, ks_ref, vst_ref, kw_ref, vwt_ref,
                     ovt_ref, o_ref, gt_ref, val_ref, pc_ref, ow_ref, s0_ref, t0_ref, s1_ref, t1_ref,
                     m_ref, acc_ref,
                     *, tq, tk, gs, n_sel):
    g0 = pl.program_id(1) * gs
    i = pl.program_id(2)
    q0 = i * tq
    nqs = NSA_HPG * tq
    gw = NSA_HPG * HEAD_DIM
    ncb = kc_ref.shape[2]
    top_n = float(min(SEL_TOPK, n_sel))
    lo, hi = _half_masks(tq)
    tq_row = q0 + lax.broadcasted_iota(jnp.int32, (1, tq), 1)
    t4 = q0 + (lax.broadcasted_iota(jnp.int32, (1, nqs), 1) & (tq - 1))

    def stack_heads(ref, u):
        parts = []
        for j in range(NSA_HPG):
            c0 = u * gw + (j // 2) * LANES
            pair = ref[:, c0:c0 + LANES]
            parts.append(jnp.where(lo if j % 2 == 0 else hi, pair, jnp.zeros_like(pair)))
        return jnp.concatenate(parts, axis=0)

    qr_s = [stack_heads(qr_ref, u) for u in range(gs)]
    bufs = ((s0_ref, t0_ref), (s1_ref, t1_ref))

    def normalized(u):
        acc = acc_ref[u]
        return acc[0:HEAD_DIM] * (1.0 / acc[HEAD_DIM:HEAD_DIM + 1])

    krow = lax.broadcasted_iota(jnp.int32, (tk, nqs), 0)
    qcol = lax.broadcasted_iota(jnp.int32, (tk, nqs), 1) & (tq - 1)
    wtiles = ((jnp.maximum(q0 - 2 * tk, 0), krow > qcol + jnp.where(i >= 2, 0, tk)),
              (jnp.maximum(q0 - tk, 0), krow >= jnp.where(i >= 1, 0, tk)),
              (q0, krow <= qcol))

    def wproduce(buf, w, u):
        k0 = pl.multiple_of(wtiles[w][0], tk)
        _flash_produce(buf, u, _dot_nt(kw_ref[0, u, pl.ds(k0, tk), :], qr_s[u]), False)

    def wconsume(buf, w, u):
        k0 = pl.multiple_of(wtiles[w][0], tk)
        _flash_consume(buf, u, wtiles[w][1], vwt_ref[0, u, :, pl.ds(k0, tk)], m_ref, acc_ref)

    _flash_reset(m_ref, acc_ref)
    for u in range(gs):
        wproduce(bufs[0], 0, u)
    for w in range(1, len(wtiles)):
        for u in range(gs):
            wproduce(bufs[w % 2], w, u)
            wconsume(bufs[(w - 1) % 2], w - 1, u)
    for u in range(gs):
        wconsume(bufs[(len(wtiles) - 1) % 2], len(wtiles) - 1, u)
        ow_ref[u] = normalized(u)

    cend = lax.broadcasted_iota(jnp.int32, (ncb, nqs), 0) * CMP_STRIDE + (CMP_BLOCK - 1)
    cmask = cend <= t4
    any_blk = t4 >= CMP_BLOCK - 1
    blk = lax.broadcasted_iota(jnp.int32, (n_sel, tq), 0)
    bt = lax.shift_right_arithmetic(tq_row, int(math.log2(SEL_BLOCK)))
    forced = (blk == 0) | (blk == bt) | (blk == bt - 1)
    ovt = ovt_ref[...]
    o_cmp = []
    for u in range(gs):
        st = jnp.where(cmask, _dot_nt(kc_ref[0, u], stack_heads(qn_ref, u)), NEG_INF)
        pc_ref[...] = jnp.exp2(st - jnp.max(st, axis=0, keepdims=True))
        l = jnp.sum(pc_ref[...], axis=0, keepdims=True)
        inv = jnp.where(any_blk, 1.0 / l, 0.0)
        o_cmp.append(_dot(vct_ref[0, u], pc_ref[...].astype(BF16)) * inv)
        pc_sum = pc_ref[:, 0:tq] * inv[:, 0:tq]
        for j in range(1, NSA_HPG):
            cols = slice(j * tq, (j + 1) * tq)
            pc_sum = pc_sum + pc_ref[:, cols] * inv[:, cols]
        imp = sum(_dot(ovt, part) for part in _split3(pc_sum))
        val_ref[u] = jnp.where(forced, SEL_FORCE, jnp.where(blk <= bt, imp[0:n_sel, :], -1.0))

    per_it = tq // SEL_BLOCK
    vals = [val_ref[u] for u in range(gs)]

    def rank_body(it, ranks):
        ranks = list(ranks)
        for r in range(per_it):
            jp = it * per_it + r
            tie = jnp.where(blk > jp, 1.0, 0.0)
            for u in range(gs):
                row = val_ref[u, pl.ds(jp, 1), :]
                ranks[u] = ranks[u] + jnp.where(row > vals[u], 1.0,
                                                jnp.where(row == vals[u], tie, 0.0))
        return tuple(ranks)

    ranks = lax.fori_loop(0, i + 1, rank_body,
                          tuple(jnp.zeros((n_sel, tq), F32) for _ in range(gs)))

    q_aug = []
    for u in range(gs):
        selb = jnp.where(ranks[u] < top_n, 0.0, NEG_INF)
        selb = jnp.concatenate([selb, jnp.zeros((LANES - n_sel, tq), F32)], axis=0).T
        selb = selb.astype(BF16)
        q_aug.append(jnp.concatenate(
            [qr_s[u], jnp.concatenate([selb] * NSA_HPG, axis=0)], axis=1))

    _flash_pipeline(
        q0 // tk, tk, gs,
        lambda u, k0: _dot_nt(ks_ref[0, u, pl.ds(k0, tk), :], q_aug[u]),
        lambda k0: k0 + lax.broadcasted_iota(jnp.int32, (tk, nqs), 0) <= t4,
        lambda u, k0: vst_ref[0, u, :, pl.ds(k0, tk)],
        ((s0_ref, t0_ref), (s1_ref, t1_ref)), m_ref, acc_ref)

    gt_ref[...] = gate_ref[...].T
    for u in range(gs):
        o_sel = normalized(u)
        o_win = ow_ref[u]

        def gate_rows(branch):
            base = branch * NSA_HEADS + (g0 + u) * NSA_HPG
            return jnp.concatenate(
                [gt_ref[pl.ds(base + j, 1), :] for j in range(NSA_HPG)], axis=1)

        o = gate_rows(0) * o_cmp[u] + gate_rows(1) * o_sel + gate_rows(2) * o_win
        o = jnp.concatenate([o[:, j * tq:(j + 1) * tq] for j in range(NSA_HPG)], axis=0)
        o_ref[:, u * gw:(u + 1) * gw] = o.T.astype(BF16)


def _nsa_attn(qn, qr, gates, kc_dup, vct, ks_aug, vst, kw_dup, vwt, ovt):
    tq, tk, gs = TQ_NSA, TK_NSA, GS_NSA
    b, g, s, _ = ks_aug.shape
    t = qn.shape[0]
    gw = NSA_HPG * HEAD_DIM
    nq = s // tq
    n_sel = s // SEL_BLOCK
    nqs = NSA_HPG * tq
    qspec = pl.BlockSpec((tq, gs * gw), lambda bi, gi, i: (bi * nq + i, gi))
    per_bg = lambda a: pl.BlockSpec((1, gs) + a.shape[2:], lambda bi, gi, i: (bi, gi, 0, 0))
    return pl.pallas_call(
        functools.partial(_nsa_attn_kernel, tq=tq, tk=tk, gs=gs, n_sel=n_sel),
        grid=(b, g // gs, nq),
        in_specs=[qspec, qspec,
                  pl.BlockSpec((tq, LANES), lambda bi, gi, i: (bi * nq + i, 0)),
                  per_bg(kc_dup), per_bg(vct), per_bg(ks_aug), per_bg(vst), per_bg(kw_dup),
                  per_bg(vwt), pl.BlockSpec(ovt.shape, lambda bi, gi, i: (0, 0))],
        out_specs=qspec,
        out_shape=jax.ShapeDtypeStruct((t, g * gw), BF16),
        scratch_shapes=[pltpu.VMEM((LANES, tq), F32),
                        pltpu.VMEM((gs, n_sel, tq), F32),
                        pltpu.VMEM((kc_dup.shape[2], nqs), F32),
                        pltpu.VMEM((gs, HEAD_DIM, nqs), F32),
                        pltpu.VMEM((gs, tk, nqs), SCORE_DTYPE), pltpu.VMEM((gs, 1, nqs), F32),
                        pltpu.VMEM((gs, tk, nqs), SCORE_DTYPE), pltpu.VMEM((gs, 1, nqs), F32),
                        pltpu.VMEM((gs, 1, nqs), F32),
                        pltpu.VMEM((gs, vst.shape[2], nqs), F32)],
        compiler_params=pltpu.CompilerParams(
            dimension_semantics=("arbitrary", "arbitrary", "arbitrary"),
            vmem_limit_bytes=VMEM_LIMIT),
        name="nsa_attn",
    )(qn, qr, gates, kc_dup, vct, ks_aug, vst, kw_dup, vwt, ovt)


def _diff_attn_kernel(lamv_ref, q_ref, k_ref, vt_ref, sg_ref, o_ref, s0_ref, t0_ref, s1_ref, t1_ref,
                      m_ref, acc_ref, *, tq, hs, lam_init):
    lv = lamv_ref[...]
    lam = (jnp.exp(jnp.sum(lv[0:1] * lv[1:2], axis=-1, keepdims=True))
           - jnp.exp(jnp.sum(lv[2:3] * lv[3:4], axis=-1, keepdims=True)) + lam_init)
    hw = 2 * HEAD_DIM
    lo, hi = _half_masks(tq)
    q_s = []
    for u in range(hs):
        q = q_ref[:, u * hw:(u + 1) * hw]
        zero = jnp.zeros_like(q)
        q_s.append(jnp.concatenate([jnp.where(lo, q, zero), jnp.where(hi, q, zero)], axis=0))

    def diag_mask(k0):
        kpos = lax.broadcasted_iota(jnp.int32, (tq, 2 * tq), 0)
        qpos = lax.broadcasted_iota(jnp.int32, (tq, 2 * tq), 1) & (tq - 1)
        return kpos <= qpos

    _flash_pipeline(
        pl.program_id(2), tq, hs,
        lambda u, k0: _dot_nt(k_ref[0, pl.ds(k0, tq), u * hw:(u + 1) * hw], q_s[u]),
        diag_mask,
        lambda u, k0: vt_ref[0, u, :, pl.ds(k0, tq)],
        ((s0_ref, t0_ref), (s1_ref, t1_ref)), m_ref, acc_ref)

    for u in range(hs):
        acc = acc_ref[u]
        o = acc[0:hw] * (1.0 / acc[hw:hw + 1])
        o = (o[:, 0:tq] - lam * o[:, tq:2 * tq]).T
        ms = jnp.mean(o * o, axis=-1, keepdims=True)
        o_ref[:, u * hw:(u + 1) * hw] = (
            (o * lax.rsqrt(ms + EPS) * sg_ref[...]) * (1.0 - lam_init)).astype(BF16)


def _diff_attn(lamv, q, k, vt, sg, lam_init):
    tq, hs = TQ_DIFF, HS_DIFF
    b, s, w = k.shape
    hw = 2 * HEAD_DIM
    heads = w // hw
    vw = vt.shape[2]
    nq = s // tq
    full = lambda shp: pl.BlockSpec(shp, lambda bi, hi, i: (0,) * len(shp))
    qspec = pl.BlockSpec((tq, hs * hw), lambda bi, hi, i: (bi * nq + i, hi))
    return pl.pallas_call(
        functools.partial(_diff_attn_kernel, tq=tq, hs=hs, lam_init=lam_init),
        grid=(b, heads // hs, nq),
        in_specs=[full(lamv.shape), qspec,
                  pl.BlockSpec((1, s, hs * hw), lambda bi, hi, i: (bi, 0, hi)),
                  pl.BlockSpec((1, hs, vw, s), lambda bi, hi, i: (bi, hi, 0, 0)),
                  full(sg.shape)],
        out_specs=qspec,
        out_shape=jax.ShapeDtypeStruct(q.shape, BF16),
        scratch_shapes=[pltpu.VMEM((hs, tq, 2 * tq), SCORE_DTYPE), pltpu.VMEM((hs, 1, 2 * tq), F32),
                        pltpu.VMEM((hs, tq, 2 * tq), SCORE_DTYPE), pltpu.VMEM((hs, 1, 2 * tq), F32),
                        pltpu.VMEM((hs, 1, 2 * tq), F32), pltpu.VMEM((hs, vw, 2 * tq), F32)],
        compiler_params=pltpu.CompilerParams(
            dimension_semantics=("arbitrary", "arbitrary", "arbitrary"),
            vmem_limit_bytes=VMEM_LIMIT),
        name="diff_attn",
    )(lamv, q, k, vt, sg)


def _mlp_kernel(x_ref, a_ref, mod_ref, gln_ref, wo_ref, w1_ref, w2_ref, o_ref,
                x1_ref, h_ref, acc_ref):
    kf = pl.program_id(1)
    m = mod_ref[0]

    @pl.when(kf == 0)
    def _():
        chunk = x_ref.shape[0] // MLP_ROW_CHUNKS
        rows = [slice(r * chunk, (r + 1) * chunk) for r in range(MLP_ROW_CHUNKS)]
        proj = [_dot(a_ref[r, :], wo_ref[...]) for r in rows]
        for r, pr in zip(rows, proj):
            x1 = x_ref[r, :] + m[2:3] * pr
            x1_ref[r, :] = x1
            h_ref[r, :] = _mod_rmsnorm(x1, gln_ref[...], m[3:4], m[4:5]).astype(BF16)
        acc_ref[...] = jnp.zeros(acc_ref.shape, F32)

    u = jnp.maximum(_dot(h_ref[...], w1_ref[...]), 0.0)
    acc_ref[...] += _dot((u * u).astype(BF16), w2_ref[...])

    @pl.when(kf == pl.num_programs(1) - 1)
    def _():
        o_ref[...] = x1_ref[...] + m[5:6] * acc_ref[...]


def _outproj_mlp(x2, attn, mod, gln, wo, w1, w2, seq):
    tm, tf = TM_MLP, TF_MLP
    t, d = x2.shape
    dff = w1.shape[1]
    return pl.pallas_call(
        _mlp_kernel,
        grid=(t // tm, dff // tf),
        in_specs=[
            pl.BlockSpec((tm, d), lambda i, k: (i, 0)),
            pl.BlockSpec((tm, d), lambda i, k: (i, 0)),
            pl.BlockSpec((1, 8, d), lambda i, k: ((i * tm) // seq, 0, 0)),
            pl.BlockSpec((1, d), lambda i, k: (0, 0)),
            pl.BlockSpec((d, d), lambda i, k: (0, 0)),
            pl.BlockSpec((d, tf), lambda i, k: (0, k)),
            pl.BlockSpec((tf, d), lambda i, k: (k, 0)),
        ],
        out_specs=pl.BlockSpec((tm, d), lambda i, k: (i, 0)),
        out_shape=jax.ShapeDtypeStruct((t, d), F32),
        scratch_shapes=[pltpu.VMEM((tm, d), F32), pltpu.VMEM((tm, d), BF16),
                        pltpu.VMEM((tm, d), F32)],
        compiler_params=pltpu.CompilerParams(
            dimension_semantics=("arbitrary", "arbitrary"), vmem_limit_bytes=VMEM_LIMIT),
        name="outproj_mlp",
    )(x2, attn, mod, gln, wo, w1, w2)


def _rope_lane_tables(positions):
    inv = ROPE_THETA ** (-jnp.arange(0, ROPE_DIM, 2, dtype=F32) / ROPE_DIM)
    ang = positions.astype(F32)[..., None] * inv
    cos, sin = jnp.cos(ang), jnp.sin(ang)
    rest = HEAD_DIM - ROPE_DIM
    ones = jnp.ones(cos.shape[:-1] + (rest,), F32)
    zeros = jnp.zeros(cos.shape[:-1] + (rest,), F32)
    cosh = jnp.concatenate([cos, cos, ones], axis=-1)
    sinh = jnp.concatenate([-sin, sin, zeros], axis=-1)
    reps = LANES // HEAD_DIM
    t = positions.shape[0] * positions.shape[1]
    return (jnp.tile(cosh, (1, 1, reps)).reshape(t, LANES),
            jnp.tile(sinh, (1, 1, reps)).reshape(t, LANES))


def _tile_gain(g, heads):
    return jnp.tile(g.astype(F32), heads)


def _dup_heads(w):
    h = w.reshape(w.shape[:-1] + (-1, 1, HEAD_DIM))
    return jnp.concatenate([h, h], axis=-2).reshape(w.shape[:-1] + (-1,))


def kernel(x, c, positions, ln_mix_g, ln_mlp_g, w_ada, b_ada, w_mlp_in, w_mlp_out, nsa_w_in, nsa_b_gate, nsa_q_gain, nsa_k_gain, nsa_pe_k, nsa_w_ck1, nsa_w_ck2, nsa_pe_v, nsa_w_cv1, nsa_w_cv2, nsa_w_out, diff_w_in, diff_q_gain, diff_k_gain, diff_lq1, diff_lk1, diff_lq2, diff_lk2, diff_subln_g, diff_w_out):
    b, s, d = x.shape
    depth = w_ada.shape[0]
    t = b * s
    n_sel = s // SEL_BLOCK
    nq, nk = NSA_HEADS * HEAD_DIM, NSA_GROUPS * HEAD_DIM
    assert s % TM_PROJ == 0 and s % TM_MLP == 0 and s % (2 * TK_NSA) == 0
    assert n_sel <= LANES and TK_NSA == TQ_NSA and WINDOW == 2 * TK_NSA

    cosf, sinf = _rope_lane_tables(positions)
    c_pad = jnp.zeros((8, d), F32).at[:b].set(c)
    mod = _ada_mod(c_pad, w_ada, b_ada)
    mod = mod[:, :b].reshape(depth, b, 6, d)
    mod = jnp.pad(mod, ((0, 0), (0, 0), (0, 2), (0, 0)))

    n_half = s // CMP_STRIDE
    cb = jnp.arange(n_half)
    sb = jnp.arange(LANES)
    ovt = ((cb[None, :] * CMP_STRIDE <= sb[:, None] * SEL_BLOCK + SEL_BLOCK - 1)
           & (cb[None, :] * CMP_STRIDE + CMP_BLOCK - 1 >= sb[:, None] * SEL_BLOCK)
           & (sb[:, None] < n_sel)).astype(BF16)

    x2 = x.reshape(t, d)
    for i in range(depth):
        j = i // 2
        gln = ln_mix_g[i].reshape(1, d)
        if i % 2 == 0:
            w = nsa_w_in[j]
            o0 = nq
            wq = w[:, :o0]
            wkc, wvc, wks, wvs, wkw, wvw = [w[:, o0 + r * nk:o0 + (r + 1) * nk] for r in range(6)]
            wgl = w[:, o0 + 6 * nk:]
            wn = jnp.concatenate([wq, wks, wkw], axis=1).astype(BF16)
            wr = jnp.concatenate([wkc, wvc, wvs, wvw], axis=1).astype(BF16)
            wg = wgl.reshape(d, NSA_HEADS, 3).transpose(0, 2, 1).reshape(d, 3 * NSA_HEADS)
            wg = jnp.pad(wg, ((0, 0), (0, LANES - 3 * NSA_HEADS))).astype(BF16)
            bg = nsa_b_gate[j].reshape(NSA_HEADS, 3).T.reshape(1, 3 * NSA_HEADS)
            bg = jnp.pad(bg, ((0, 0), (0, LANES - 3 * NSA_HEADS)))
            gain = jnp.concatenate([_tile_gain(nsa_q_gain[j], NSA_HEADS),
                                    _tile_gain(nsa_k_gain[j, 1], NSA_GROUPS),
                                    _tile_gain(nsa_k_gain[j, 2], NSA_GROUPS)]).reshape(1, -1)
            qn, qr, ks_aug, kw_dup, kcv, vst, vwt, gates = _nsa_proj(
                x2, mod[i], gln, wn, wr, wg, bg, gain, cosf, sinf, b, s)

            def pair_blockdiag(w):
                z = jnp.zeros_like(w)
                return jnp.concatenate([jnp.concatenate([w, z], axis=-1),
                                        jnp.concatenate([z, w], axis=-1)], axis=-2)

            def w1_stack(w1):
                return pair_blockdiag(w1.reshape(CMP_BLOCK, HEAD_DIM, CMP_HIDDEN)).astype(BF16)

            kc_dup, vct = _nsa_compress(
                kcv, b, s, jnp.tile(nsa_pe_k[j], (1, 2)), jnp.tile(nsa_pe_v[j], (1, 2)),
                w1_stack(nsa_w_ck1[j]), pair_blockdiag(_dup_heads(nsa_w_ck2[j])).astype(BF16),
                w1_stack(nsa_w_cv1[j]), nsa_w_cv2[j].T.astype(BF16),
                _tile_gain(nsa_k_gain[j, 0], 2).reshape(1, 2 * HEAD_DIM))
            attn = _nsa_attn(qn, qr, gates, kc_dup, vct, ks_aug, vst, kw_dup, vwt, ovt)
            wo = nsa_w_out[j]
        else:
            w = diff_w_in[j]
            dw = w.shape[1] // 3
            heads = dw // HEAD_DIM
            wn = w[:, :2 * dw].astype(BF16)
            wr = w[:, 2 * dw:].astype(BF16)
            gain = jnp.concatenate([_tile_gain(diff_q_gain[j], heads),
                                    _tile_gain(diff_k_gain[j], heads)]).reshape(1, -1)
            q, k, vt = _diff_proj(x2, mod[i], gln, wn, wr, gain, cosf, sinf, b, s)
            lamv = jnp.stack([diff_lq1[j], diff_lk1[j], diff_lq2[j], diff_lk2[j]]).astype(F32)
            lamv = jnp.pad(lamv, ((0, 4), (0, LANES - HEAD_DIM)))
            lam_init = 0.8 - 0.6 * math.exp(-0.3 * i)
            attn = _diff_attn(lamv, q, k, vt, diff_subln_g[j].reshape(1, -1), lam_init)
            wo = diff_w_out[j]
        x2 = _outproj_mlp(x2, attn, mod[i], ln_mlp_g[i].reshape(1, d), wo.astype(BF16),
                          w_mlp_in[i].astype(BF16), w_mlp_out[i].astype(BF16), s)
    return x2.reshape(b, s, d)
```
